```python
import jax, jax.numpy as jnp
from jax import lax
import numpy as np

D_MODEL = 4096
BATCH = 4
SEQ = 2048
DEPTH = 1
DEC_BATCH = 32
DEC_SEQ = 8
PAST_LEN = 8192
PAGE_SIZE = 128

HEAD_DIM = 128
SB_HEADS = 16
NSA_HEADS = 16
NSA_KV_GROUPS = 2
NSA_REP = NSA_HEADS // NSA_KV_GROUPS
CMP_STRIDE = 16
CMP_BLOCK = 2 * CMP_STRIDE
CMP_HIDDEN = 256
SEL_BLOCK = 64
SEL_TOPK = 16
WINDOW = 512
Q_BLOCK = 128
MOE_GROUPS = 4
EXPERTS_PER_GROUP = 8
N_EXPERTS = MOE_GROUPS * EXPERTS_PER_GROUP
TOPK_IN_GROUP = 2
EXPERT_FF = 512
SB_WIDTH = SB_HEADS * HEAD_DIM
NSA_WIDTH = NSA_HEADS * HEAD_DIM
KV_WIDTH = NSA_KV_GROUPS * HEAD_DIM
IN_COLS = 3 * SB_WIDTH + NSA_WIDTH + 6 * KV_WIDTH + 3 * NSA_HEADS + 2 * D_MODEL
NORM_EPS = 1e-6
NEG_INF = -1e30
FORCE_SCORE = 1e4

kernel_name = "sb_nsa_hier_moe_decode_step"


def rms_norm(x, g):
    x32 = x.astype(jnp.float32)
    y = x32 * lax.rsqrt(jnp.mean(x32 * x32, axis=-1, keepdims=True) + NORM_EPS)
    return (y * g.astype(jnp.float32)).astype(x.dtype)


def masked_softmax(s, mask):
    p = jax.nn.softmax(jnp.where(mask, s, NEG_INF), axis=-1)
    return jnp.where(mask, p, 0.0)


def alibi_slopes():
    h = jnp.arange(1, NSA_HEADS + 1, dtype=jnp.float32)
    return jnp.exp2(-8.0 * h / NSA_HEADS).reshape(NSA_KV_GROUPS, NSA_REP)


def adaln_modulation(c, w_ada, b_ada):
    mod = jax.nn.silu(c) @ w_ada + b_ada
    return jnp.split(mod[:, None, :], 6, axis=-1)


def modulate(x, g, shift, scale):
    return rms_norm(x, g) * (1.0 + scale) + shift


def stick_breaking_weights(z, mask):
    sp = jnp.where(mask, jax.nn.softplus(z), 0.0)
    later = lax.cumsum(sp, axis=z.ndim - 1, reverse=True) - sp
    return jnp.where(mask, jnp.exp(jax.nn.log_sigmoid(z) - later), 0.0)


def project(h, w):
    B, T, _ = h.shape
    p = jnp.einsum("btd,de->bte", h, w["w_in"])
    offs = np.cumsum([SB_WIDTH, SB_WIDTH, SB_WIDTH, NSA_WIDTH, 6 * KV_WIDTH, 3 * NSA_HEADS, D_MODEL]).tolist()
    q_sb, k_sb, v_sb, q_n, kv_n, g_n, gate_sb, gate_n = jnp.split(p, offs, axis=-1)
    sb_kv = jnp.stack([k_sb.reshape(B, T, SB_HEADS, HEAD_DIM), v_sb.reshape(B, T, SB_HEADS, HEAD_DIM)], axis=2)
    q_sb = q_sb.reshape(B, T, SB_HEADS, HEAD_DIM)
    q_n = rms_norm(q_n.reshape(B, T, NSA_KV_GROUPS, NSA_REP, HEAD_DIM), w["nsa_q_g"])
    kv_n = kv_n.reshape(B, T, 6, NSA_KV_GROUPS, HEAD_DIM)
    nsa_kv = jnp.stack([kv_n[:, :, 0], kv_n[:, :, 1], rms_norm(kv_n[:, :, 2], w["nsa_ks_g"]), kv_n[:, :, 3]], axis=2)
    win_kv = jnp.stack([rms_norm(kv_n[:, :, 4], w["nsa_kw_g"]), kv_n[:, :, 5]], axis=2)
    g_n = jax.nn.sigmoid(g_n.reshape(B, T, NSA_KV_GROUPS, NSA_REP, 3).astype(jnp.float32))
    return q_sb, sb_kv, q_n, nsa_kv, win_kv, g_n, gate_sb, gate_n


def compress_blocks(x, pos_emb, w1, w2):
    B, L, G, D = x.shape
    n = L // CMP_STRIDE
    chunks = x.reshape(B, n, CMP_STRIDE, G, D)
    nxt = jnp.concatenate([chunks[:, 1:], jnp.zeros_like(chunks[:, :1])], axis=1)
    blocks = jnp.concatenate([chunks, nxt], axis=2) + pos_emb[:, None, :]
    flat = jnp.moveaxis(blocks, 3, 2).reshape(B, n, G, CMP_BLOCK * D)
    return jax.nn.gelu(flat @ w1) @ w2


def select_blocks(p_cmp, q_pos):
    B, G, R, T, N = p_cmp.shape
    ratio = SEL_BLOCK // CMP_STRIDE
    n_sel = N // ratio
    score = p_cmp.sum(axis=2).reshape(B, G, T, n_sel, ratio).sum(axis=-1)
    j = jnp.arange(n_sel)[None, :]
    cur = (q_pos // SEL_BLOCK)[:, None]
    forced = (j == 0) | (j == cur) | (j == cur - 1)
    score = jnp.where(j > cur, NEG_INF, jnp.where(forced, FORCE_SCORE, score))
    _, idx = lax.top_k(score, min(SEL_TOPK, n_sel))
    return idx


def nsa_compressed_and_select(q, q_pos, cmp_k, cmp_v, w, slopes):
    k_c = rms_norm(compress_blocks(cmp_k, w["cmp_pos_k"], w["cmp_w1_k"], w["cmp_w2_k"]), w["nsa_kc_g"])
    v_c = compress_blocks(cmp_v, w["cmp_pos_v"], w["cmp_w1_v"], w["cmp_w2_v"])
    end = jnp.arange(k_c.shape[1]) * CMP_STRIDE + (CMP_BLOCK - 1)
    dist = (q_pos[:, None] - end[None, :]).astype(jnp.float32)
    s = jnp.einsum("btgrd,bngd->bgrtn", q.astype(jnp.float32), k_c.astype(jnp.float32)) * HEAD_DIM ** -0.5
    s = s - slopes[None, :, :, None, None] * dist
    p = masked_softmax(s, dist >= 0)
    o = jnp.einsum("bgrtn,bngd->btgrd", p, v_c.astype(jnp.float32))
    return o, select_blocks(p, q_pos)


def sel_view(x):
    B, L, G, D = x.shape
    return jnp.transpose(x.reshape(B, L // SEL_BLOCK, SEL_BLOCK, G, D), (0, 3, 1, 2, 4))


def nsa_selected(q, q_pos, k_blk, v_blk, idx, slopes):
    B, G, T, n = idx.shape
    bi = jnp.arange(B)[:, None, None, None]
    gi = jnp.arange(G)[None, :, None, None]
    kg = k_blk[bi, gi, idx].reshape(B, G, T, n * SEL_BLOCK, HEAD_DIM)
    vg = v_blk[bi, gi, idx].reshape(B, G, T, n * SEL_BLOCK, HEAD_DIM)
    k_pos = (idx[..., None] * SEL_BLOCK + jnp.arange(SEL_BLOCK)).reshape(B, G, T, n * SEL_BLOCK)
    dist = (q_pos[None, None, :, None] - k_pos).astype(jnp.float32)[:, :, None]
    s = jnp.einsum("btgrd,bgtkd->bgrtk", q.astype(jnp.float32), kg.astype(jnp.float32)) * HEAD_DIM ** -0.5
    s = s - slopes[None, :, :, None, None] * dist
    p = masked_softmax(s, dist >= 0)
    return jnp.einsum("bgrtk,bgtkd->btgrd", p, vg.astype(jnp.float32))


def nsa_window(q, q_pos, k, v, k_pos, slopes):
    dist = (q_pos[:, None] - k_pos[None, :]).astype(jnp.float32)
    s = jnp.einsum("btgrd,bsgd->bgrts", q.astype(jnp.float32), k.astype(jnp.float32)) * HEAD_DIM ** -0.5
    s = s - slopes[None, :, :, None, None] * dist
    mask = (dist >= 0) & (dist < WINDOW) & (k_pos[None, :] >= 0)
    p = masked_softmax(s, mask)
    return jnp.einsum("bgrts,bsgd->btgrd", p, v.astype(jnp.float32))


def window_prompt(q, k, v, slopes):
    B, T, G, R, D = q.shape
    nb = T // Q_BLOCK
    nband = WINDOW // Q_BLOCK + 1
    kp = jnp.pad(k, ((0, 0), (WINDOW, 0), (0, 0), (0, 0))).reshape(B, nb + nband - 1, Q_BLOCK, G, D)
    vp = jnp.pad(v, ((0, 0), (WINDOW, 0), (0, 0), (0, 0))).reshape(B, nb + nband - 1, Q_BLOCK, G, D)
    pos = jnp.arange(-WINDOW, T).reshape(nb + nband - 1, Q_BLOCK)
    band = jnp.arange(nb)[:, None] + jnp.arange(nband)[None, :]
    kb = jnp.moveaxis(kp[:, band].reshape(B, nb, nband * Q_BLOCK, G, D), 1, 0)
    vb = jnp.moveaxis(vp[:, band].reshape(B, nb, nband * Q_BLOCK, G, D), 1, 0)
    pb = pos[band].reshape(nb, nband * Q_BLOCK)
    qb = jnp.moveaxis(q.reshape(B, nb, Q_BLOCK, G, R, D), 1, 0)
    qpos = jnp.arange(T).reshape(nb, Q_BLOCK)
    out = lax.map(lambda a: nsa_window(a[0], a[1], a[2], a[3], a[4], slopes), (qb, qpos, kb, vb, pb))
    return jnp.moveaxis(out, 0, 1).reshape(B, T, G, R, D)


def merge_branches(o_sb, o_nsa, gate_sb, gate_n, w):
    B, T = o_sb.shape[:2]
    u_sb = o_sb.reshape(B, T, SB_WIDTH) @ w["w_br_sb"]
    u_n = o_nsa.reshape(B, T, NSA_WIDTH) @ w["w_br_nsa"]
    mixed = jax.nn.sigmoid(gate_sb) * u_sb + jax.nn.sigmoid(gate_n) * u_n
    return mixed @ w["w_out"]


def combine_nsa(g_n, o_c, o_s, o_w, dtype):
    return (g_n[..., 0:1] * o_c + g_n[..., 1:2] * o_s + g_n[..., 2:3] * o_w).astype(dtype)


def prompt_mixer(h, w, slopes):
    B, T, _ = h.shape
    q_sb, sb_kv, q_n, nsa_kv, win_kv, g_n, gate_sb, gate_n = project(h, w)
    nb = T // Q_BLOCK
    pos = jnp.arange(T)
    qpos_b = pos.reshape(nb, Q_BLOCK)
    to_blocks = lambda a: jnp.moveaxis(a.reshape((B, nb, Q_BLOCK) + a.shape[2:]), 1, 0)
    from_blocks = lambda a: jnp.moveaxis(a, 0, 1).reshape((B, T) + a.shape[3:])
    k_sb = sb_kv[:, :, 0].astype(jnp.float32)
    v_sb = sb_kv[:, :, 1].astype(jnp.float32)

    def sb_block(args):
        qb, qp = args
        z = jnp.einsum("bthd,bshd->bhts", qb.astype(jnp.float32), k_sb) * HEAD_DIM ** -0.5
        a = stick_breaking_weights(z, pos[None, :] < qp[:, None])
        return jnp.einsum("bhts,bshd->bthd", a, v_sb)

    o_sb = from_blocks(lax.map(sb_block, (to_blocks(q_sb), qpos_b))).astype(h.dtype)
    o_c, idx = nsa_compressed_and_select(q_n, pos, nsa_kv[:, :, 0], nsa_kv[:, :, 1], w, slopes)
    k_blk = sel_view(nsa_kv[:, :, 2])
    v_blk = sel_view(nsa_kv[:, :, 3])
    idx_b = jnp.moveaxis(idx.reshape(B, NSA_KV_GROUPS, nb, Q_BLOCK, idx.shape[-1]), 2, 0)
    o_s = from_blocks(lax.map(lambda a: nsa_selected(a[0], a[1], k_blk, v_blk, a[2], slopes),
                              (to_blocks(q_n), qpos_b, idx_b)))
    o_w = window_prompt(q_n, win_kv[:, :, 0], win_kv[:, :, 1], slopes)
    o_n = combine_nsa(g_n, o_c, o_s, o_w, h.dtype)
    mix = merge_branches(o_sb, o_n, gate_sb, gate_n, w)
    return mix, (sb_kv, nsa_kv, win_kv[:, T - min(WINDOW, T):])


def sample_mixer(h, cache_sb_kv, cache_nsa_kv, cache_win_kv, page_table, layer, w, slopes):
    B, T, _ = h.shape
    q_sb, sb_kv, q_n, nsa_kv, win_kv, g_n, gate_sb, gate_n = project(h, w)
    past = page_table.shape[1] * PAGE_SIZE
    q_pos = past + jnp.arange(T)
    k_past = cache_sb_kv[layer, page_table, :, 0].reshape(B, past, SB_HEADS, HEAD_DIM).astype(jnp.float32)
    v_past = cache_sb_kv[layer, page_table, :, 1].reshape(B, past, SB_HEADS, HEAD_DIM).astype(jnp.float32)
    q32 = q_sb.astype(jnp.float32)
    k_new = sb_kv[:, :, 0].astype(jnp.float32)
    v_new = sb_kv[:, :, 1].astype(jnp.float32)
    z = jnp.concatenate([jnp.einsum("bthd,bshd->bhts", q32, k_past),
                         jnp.einsum("bthd,bshd->bhts", q32, k_new)], axis=-1) * HEAD_DIM ** -0.5
    k_pos = jnp.arange(past + T)
    a = stick_breaking_weights(z, k_pos[None, :] < q_pos[:, None])
    o_sb = (jnp.einsum("bhts,bshd->bthd", a[..., :past], v_past)
            + jnp.einsum("bhts,bshd->bthd", a[..., past:], v_new)).astype(h.dtype)
    past_n = cache_nsa_kv[layer, page_table].reshape(B, past, 4, NSA_KV_GROUPS, HEAD_DIM)
    pad = (-(past + T)) % SEL_BLOCK
    all_n = jnp.concatenate([past_n, nsa_kv, jnp.zeros((B, pad, 4, NSA_KV_GROUPS, HEAD_DIM), nsa_kv.dtype)], axis=1)
    o_c, idx = nsa_compressed_and_select(q_n, q_pos, all_n[:, :, 0], all_n[:, :, 1], w, slopes)
    o_s = nsa_selected(q_n, q_pos, sel_view(all_n[:, :, 2]), sel_view(all_n[:, :, 3]), idx, slopes)
    buf_in = cache_win_kv[layer]
    nbuf = buf_in.shape[1]
    buf = jnp.concatenate([buf_in, win_kv], axis=1)
    w_pos = past - nbuf + jnp.arange(nbuf + T)
    o_w = nsa_window(q_n, q_pos, buf[:, :, 0], buf[:, :, 1], w_pos, slopes)
    o_n = combine_nsa(g_n, o_c, o_s, o_w, h.dtype)
    mix = merge_branches(o_sb, o_n, gate_sb, gate_n, w)
    return mix, (sb_kv, nsa_kv, buf[:, T:])


def hierarchical_moe(h, w):
    B, T, D = h.shape
    t = h.reshape(B * T, D)
    t32 = t.astype(jnp.float32)
    pg = jax.nn.softmax(t32 @ w["w_route_group"] + w["b_route_group"], axis=-1)
    g_top, g_idx = lax.top_k(pg, 1)
    le = (t32 @ w["w_route_expert"] + w["b_route_expert"]).reshape(B * T, MOE_GROUPS, EXPERTS_PER_GROUP)
    le_g = jnp.take_along_axis(le, g_idx[:, :, None], axis=1)[:, 0]
    pe = jax.nn.softmax(le_g, axis=-1)
    e_top, e_idx = lax.top_k(pe, TOPK_IN_GROUP)
    e_w = e_top / jnp.sum(e_top, axis=-1, keepdims=True) * g_top
    comb_in = jnp.sum(jax.nn.one_hot(e_idx, EXPERTS_PER_GROUP) * e_w[..., None], axis=1)
    comb = (jax.nn.one_hot(g_idx[:, 0], MOE_GROUPS)[:, :, None] * comb_in[:, None, :]).reshape(B * T, N_EXPERTS)
    hid = jax.nn.silu(jnp.einsum("nd,edf->nef", t, w["w_exp_gate"])) * jnp.einsum("nd,edf->nef", t, w["w_exp_up"])
    y = jnp.einsum("nef,efd->nd", hid * comb[..., None].astype(hid.dtype), w["w_exp_down"])
    return y.reshape(B, T, D).astype(h.dtype)


def run_layer(x, c, w, mixer):
    shift1, scale1, gate1, shift2, scale2, gate2 = adaln_modulation(c, w["w_ada"], w["b_ada"])
    h = modulate(x, w["norm1_g"], shift1, scale1)
    mix, state = mixer(h)
    x = x + gate1 * mix
    h = modulate(x, w["norm2_g"], shift2, scale2)
    x = x + gate2 * hierarchical_moe(h, w)
    return x, state


def setup_inputs(seed: int = 0) -> dict:
    key = jax.random.key(seed)
    ks = iter(jax.random.split(key, 40))
    nrm = lambda shape, scale: jax.random.normal(next(ks), shape, jnp.float32) * scale
    n_pages = PAST_LEN // PAGE_SIZE
    n_used = DEC_BATCH * n_pages
    n_pool = n_used + (n_used + 3) // 4
    win_buf = min(WINDOW, PAST_LEN)
    G = NSA_KV_GROUPS
    return {
        "x_prompt": nrm((BATCH, SEQ, D_MODEL), 1.0),
        "x_sample": nrm((DEC_BATCH, DEC_SEQ, D_MODEL), 1.0),
        "c_prompt": nrm((BATCH, D_MODEL), 1.0),
        "c_sample": nrm((DEC_BATCH, D_MODEL), 1.0),
        "cache_sb_kv": nrm((DEPTH, n_pool, PAGE_SIZE, 2, SB_HEADS, HEAD_DIM), 1.0),
        "cache_nsa_kv": nrm((DEPTH, n_pool, PAGE_SIZE, 4, G, HEAD_DIM), 1.0),
        "cache_win_kv": nrm((DEPTH, DEC_BATCH, win_buf, 2, G, HEAD_DIM), 1.0),
        "page_table": jax.random.permutation(next(ks), n_pool)[:n_used].reshape(DEC_BATCH, n_pages).astype(jnp.int32),
        "norm1_g": 1.0 + nrm((DEPTH, D_MODEL), 0.05),
        "norm2_g": 1.0 + nrm((DEPTH, D_MODEL), 0.05),
        "w_ada": nrm((DEPTH, D_MODEL, 6 * D_MODEL), 0.5 * D_MODEL ** -0.5),
        "b_ada": nrm((DEPTH, 6 * D_MODEL), 0.01),
        "w_in": nrm((DEPTH, D_MODEL, IN_COLS), D_MODEL ** -0.5),
        "nsa_q_g": 1.0 + nrm((DEPTH, HEAD_DIM), 0.05),
        "nsa_kc_g": 1.0 + nrm((DEPTH, HEAD_DIM), 0.05),
        "nsa_ks_g": 1.0 + nrm((DEPTH, HEAD_DIM), 0.05),
        "nsa_kw_g": 1.0 + nrm((DEPTH, HEAD_DIM), 0.05),
        "cmp_pos_k": nrm((DEPTH, CMP_BLOCK, HEAD_DIM), 0.1),
        "cmp_pos_v": nrm((DEPTH, CMP_BLOCK, HEAD_DIM), 0.1),
        "cmp_w1_k": nrm((DEPTH, CMP_BLOCK * HEAD_DIM, CMP_HIDDEN), (CMP_BLOCK * HEAD_DIM) ** -0.5),
        "cmp_w2_k": nrm((DEPTH, CMP_HIDDEN, HEAD_DIM), CMP_HIDDEN ** -0.5),
        "cmp_w1_v": nrm((DEPTH, CMP_BLOCK * HEAD_DIM, CMP_HIDDEN), (CMP_BLOCK * HEAD_DIM) ** -0.5),
        "cmp_w2_v": nrm((DEPTH, CMP_HIDDEN, HEAD_DIM), CMP_HIDDEN ** -0.5),
        "w_br_sb": nrm((DEPTH, SB_WIDTH, D_MODEL), SB_WIDTH ** -0.5),
        "w_br_nsa": nrm((DEPTH, NSA_WIDTH, D_MODEL), NSA_WIDTH ** -0.5),
        "w_out": nrm((DEPTH, D_MODEL, D_MODEL), D_MODEL ** -0.5),
        "w_route_group": nrm((DEPTH, D_MODEL, MOE_GROUPS), D_MODEL ** -0.5),
        "b_route_group": nrm((DEPTH, MOE_GROUPS), 0.01),
        "w_route_expert": nrm((DEPTH, D_MODEL, N_EXPERTS), D_MODEL ** -0.5),
        "b_route_expert": nrm((DEPTH, N_EXPERTS), 0.01),
        "w_exp_gate": nrm((DEPTH, N_EXPERTS, D_MODEL, EXPERT_FF), D_MODEL ** -0.5),
        "w_exp_up": nrm((DEPTH, N_EXPERTS, D_MODEL, EXPERT_FF), D_MODEL ** -0.5),
        "w_exp_down": nrm((DEPTH, N_EXPERTS, EXPERT_FF, D_MODEL), EXPERT_FF ** -0.5),
    }


def reference(x_prompt, x_sample, c_prompt, c_sample, cache_sb_kv, cache_nsa_kv, cache_win_kv, page_table,
              norm1_g, norm2_g, w_ada, b_ada, w_in, nsa_q_g, nsa_kc_g, nsa_ks_g, nsa_kw_g,
              cmp_pos_k, cmp_pos_v, cmp_w1_k, cmp_w2_k, cmp_w1_v, cmp_w2_v,
              w_br_sb, w_br_nsa, w_out, w_route_group, b_route_group, w_route_expert, b_route_expert,
              w_exp_gate, w_exp_up, w_exp_down):
    slopes = alibi_slopes()
    xp, xs = x_prompt, x_sample
    sb_p, nsa_p, win_p, sb_s, nsa_s, win_s = [], [], [], [], [], []
    for layer in range(DEPTH):
        w = {
            "norm1_g": norm1_g[layer], "norm2_g": norm2_g[layer], "w_ada": w_ada[layer], "b_ada": b_ada[layer],
            "w_in": w_in[layer], "nsa_q_g": nsa_q_g[layer], "nsa_kc_g": nsa_kc_g[layer],
            "nsa_ks_g": nsa_ks_g[layer], "nsa_kw_g": nsa_kw_g[layer],
            "cmp_pos_k": cmp_pos_k[layer], "cmp_pos_v": cmp_pos_v[layer],
            "cmp_w1_k": cmp_w1_k[layer], "cmp_w2_k": cmp_w2_k[layer],
            "cmp_w1_v": cmp_w1_v[layer], "cmp_w2_v": cmp_w2_v[layer],
            "w_br_sb": w_br_sb[layer], "w_br_nsa": w_br_nsa[layer], "w_out": w_out[layer],
            "w_route_group": w_route_group[layer], "b_route_group": b_route_group[layer],
            "w_route_expert": w_route_expert[layer], "b_route_expert": b_route_expert[layer],
            "w_exp_gate": w_exp_gate[layer], "w_exp_up": w_exp_up[layer], "w_exp_down": w_exp_down[layer],
        }
        xp, st_p = run_layer(xp, c_prompt, w, lambda h: prompt_mixer(h, w, slopes))
        xs, st_s = run_layer(xs, c_sample, w, lambda h: sample_mixer(h, cache_sb_kv, cache_nsa_kv, cache_win_kv,
                                                                  page_table, layer, w, slopes))
        sb_p.append(st_p[0]); nsa_p.append(st_p[1]); win_p.append(st_p[2])
        sb_s.append(st_s[0]); nsa_s.append(st_s[1]); win_s.append(st_s[2])
    new_sb_kv_prompt = jnp.stack(sb_p)
    new_nsa_kv_prompt = jnp.stack(nsa_p)
    new_win_kv_prompt = jnp.stack(win_p)
    new_sb_kv_sample = jnp.stack(sb_s)
    new_nsa_kv_sample = jnp.stack(nsa_s)
    new_win_kv_sample = jnp.stack(win_s)
    return (xp, xs, new_sb_kv_prompt, new_nsa_kv_prompt, new_win_kv_prompt,
            new_sb_kv_sample, new_nsa_kv_sample, new_win_kv_sample)
```

```python
import functools

import jax
import jax.numpy as jnp
from jax import lax
from jax.experimental import pallas as pl
from jax.experimental.pallas import tpu as pltpu

F32 = jnp.float32
BF16 = jnp.bfloat16
I32 = jnp.int32

HEAD_DIM = 128
SB_HEADS = 16
NSA_HEADS = 16
NSA_G = 2
NSA_REP = NSA_HEADS // NSA_G
CMP_STRIDE = 16
CMP_BLOCK = 2 * CMP_STRIDE
CMP_HIDDEN = 256
SEL_BLOCK = 64
SEL_RATIO = SEL_BLOCK // CMP_STRIDE
SEL_TOPK = 16
WINDOW = 512
PAGE = 128
MOE_GROUPS = 4
EPG = 8
N_EXPERTS = MOE_GROUPS * EPG
EXPERT_FF = 512
NORM_EPS = 1e-6
NEG_INF = -1e30
FORCE_SCORE = 1e4
LOWEST = -3e38
SCALE = HEAD_DIM ** -0.5
LANES = 128
VMEM_LIMIT = 48 * 1024 * 1024


def _cparams(*sem):
    return pltpu.CompilerParams(dimension_semantics=sem, vmem_limit_bytes=VMEM_LIMIT)


def _dot(a, b):
    return jnp.dot(a, b, preferred_element_type=F32)


def _dot_t(a, b):
    return lax.dot_general(a, b, (((1,), (1,)), ((), ())), preferred_element_type=F32)


def _split(x):
    hi = x.astype(BF16)
    lo = (x - hi.astype(F32)).astype(BF16)
    return hi, lo


def _dot3(a, b):
    a_hi, a_lo = _split(a)
    b_hi, b_lo = _split(b)
    return _dot(a_hi, b_hi) + _dot(a_lo, b_hi) + _dot(a_hi, b_lo)


def _dot2_exact_rhs(a, b_bf16):
    a_hi, a_lo = _split(a)
    return _dot(a_hi, b_bf16) + _dot(a_lo, b_bf16)


def _sigmoid(x):
    return 1.0 / (1.0 + jnp.exp(-x))


def _softplus(x):
    return jnp.maximum(x, 0.0) + jnp.log(1.0 + jnp.exp(-jnp.abs(x)))


def _rms(x, gain):
    return x * lax.rsqrt(jnp.mean(x * x, axis=-1, keepdims=True) + NORM_EPS) * gain


def _masked_softmax(s, mask):
    s = jnp.where(mask, s, NEG_INF)
    m = jnp.max(s, axis=-1, keepdims=True)
    e = jnp.where(mask, jnp.exp(s - m), 0.0)
    l = jnp.sum(e, axis=-1, keepdims=True)
    return e / jnp.where(l > 0.0, l, 1.0)


def _as2d(v):
    return v.reshape(v.shape[-2], v.shape[-1])


def _adaln_kernel(c_ref, w_ref, b_ref, o_ref):
    c = c_ref[...]
    o_ref[...] = _dot3(c * _sigmoid(c), w_ref[...]) + b_ref[...]


def _adaln(c_pad, w_ada, b_ada):
    m, d = c_pad.shape
    n = w_ada.shape[1]
    tn = 512
    return pl.pallas_call(
        _adaln_kernel,
        out_shape=jax.ShapeDtypeStruct((m, n), F32),
        grid=(n // tn,),
        in_specs=[pl.BlockSpec((m, d), lambda j: (0, 0)),
                  pl.BlockSpec((d, tn), lambda j: (0, j)),
                  pl.BlockSpec((1, tn), lambda j: (0, j))],
        out_specs=pl.BlockSpec((m, tn), lambda j: (0, j)),
        compiler_params=_cparams("arbitrary"),
        name="adaln",
    )(c_pad, w_ada, b_ada.reshape(1, n))


def _modulate_kernel(x_ref, g_ref, scale_ref, shift_ref, h_ref):
    h = _rms(x_ref[...], g_ref[...]) * (1.0 + _as2d(scale_ref[...])) + _as2d(shift_ref[...])
    h_ref[...] = h.astype(h_ref.dtype)


def _mod_spec(per_token, tm, d, col, nt):
    if per_token:
        return pl.BlockSpec((tm, d), lambda b, i: (b * nt + i, col))
    return pl.BlockSpec((1, 1, d), lambda b, i: (b, 0, col))


def _modulate(x, gain, mod, per_token, nb, scale_col, shift_col, tm):
    m, d = x.shape
    nt = m // nb // tm
    row = pl.BlockSpec((tm, d), lambda b, i: (b * nt + i, 0))
    return pl.pallas_call(
        _modulate_kernel,
        out_shape=jax.ShapeDtypeStruct((m, d), BF16),
        grid=(nb, nt),
        in_specs=[row, pl.BlockSpec((1, d), lambda b, i: (0, 0)),
                  _mod_spec(per_token, tm, d, scale_col, nt), _mod_spec(per_token, tm, d, shift_col, nt)],
        out_specs=row,
        compiler_params=_cparams("arbitrary", "arbitrary"),
        name="modulate",
    )(x, gain.reshape(1, d), mod, mod)


def _mm_kernel(*refs, n_pairs, n_extra, epilogue):
    xs = refs[:n_pairs]
    ws = refs[n_pairs:2 * n_pairs]
    extras = refs[2 * n_pairs:2 * n_pairs + n_extra]
    o_ref = refs[2 * n_pairs + n_extra]
    wbfs = refs[2 * n_pairs + n_extra + 1:]

    @pl.when(pl.program_id(1) == 0)
    def _():
        for w, wb in zip(ws, wbfs):
            wb[...] = w[...].astype(BF16)

    accs = [_dot(x[...], wb[...]) for x, wb in zip(xs, wbfs)]
    o_ref[...] = epilogue(accs, [e[...] for e in extras]).astype(o_ref.dtype)


def _matmul(xs, ws, col_off, n_cols, extras, epilogue, out_dtype, tm, tn, name):
    m = xs[0].shape[0]
    assert m % tm == 0 and n_cols % tn == 0 and col_off % tn == 0
    joff = col_off // tn
    in_specs = [pl.BlockSpec((tm, x.shape[1]), lambda j, i: (i, 0)) for x in xs]
    in_specs += [pl.BlockSpec((w.shape[0], tn), lambda j, i: (0, joff + j)) for w in ws]
    for kind, arr in extras:
        if kind == "col":
            in_specs.append(pl.BlockSpec((1, tn), lambda j, i: (0, j)))
        else:
            in_specs.append(pl.BlockSpec((tm, tn), lambda j, i: (i, j)))
    return pl.pallas_call(
        functools.partial(_mm_kernel, n_pairs=len(xs), n_extra=len(extras), epilogue=epilogue),
        out_shape=jax.ShapeDtypeStruct((m, n_cols), out_dtype),
        grid=(n_cols // tn, m // tm),
        in_specs=in_specs,
        out_specs=pl.BlockSpec((tm, tn), lambda j, i: (i, j)),
        scratch_shapes=[pltpu.VMEM((w.shape[0], tn), BF16) for w in ws],
        compiler_params=_cparams("arbitrary", "arbitrary"),
        name=name,
    )(*xs, *ws, *[a for _, a in extras])


def _ep_plain(accs, extras):
    return accs[0]


def _ep_scaled(accs, extras):
    return accs[0] * SCALE


def _ep_sigmoid(accs, extras):
    return _sigmoid(accs[0])


def _ep_gated_pair(accs, extras):
    return extras[0].astype(F32) * accs[0] + extras[1].astype(F32) * accs[1]


def _ep_groupnorm(accs, extras, post_scale):
    acc = accs[0]
    gain, flag = extras
    outs = []
    for c in range(acc.shape[1] // LANES):
        sl = slice(c * LANES, (c + 1) * LANES)
        x = acc[:, sl]
        outs.append(jnp.where(flag[:, sl] > 0.0, _rms(x, gain[:, sl]), x) * post_scale)
    return jnp.concatenate(outs, axis=1)


def _upper_ones(n):
    j = lax.broadcasted_iota(I32, (n, n), 0)
    s = lax.broadcasted_iota(I32, (n, n), 1)
    return jnp.where(j >= s, 1.0, 0.0).astype(BF16)


def _sb_block(q, k, v, ones_u, later, mask):
    z = _dot_t(q, k)
    sp = _softplus(z)
    if mask is not None:
        sp = jnp.where(mask, sp, 0.0)
    csum = _dot2_exact_rhs(sp, ones_u)
    a = jnp.exp(z - csum - later)
    if mask is not None:
        a = jnp.where(mask, a, 0.0)
    return _dot(a.astype(BF16), v), later + csum[:, 0:1]


def _sb_prompt_kernel(q_ref, k_ref, v_ref, o_ref, *, tq):
    qi = pl.program_id(2)
    q = q_ref[...]
    ones_u = _upper_ones(tq)
    row = lax.broadcasted_iota(I32, (tq, tq), 0)
    col = lax.broadcasted_iota(I32, (tq, tq), 1)

    def kv(kj):
        start = pl.multiple_of(kj * tq, tq)
        return k_ref[pl.ds(start, tq), :].astype(BF16), v_ref[pl.ds(start, tq), :].astype(BF16)

    k, v = kv(qi)
    acc, later = _sb_block(q, k, v, ones_u, jnp.zeros((tq, 1), F32), col < row)

    def body(step, carry):
        acc, later = carry
        k, v = kv(qi - 1 - step)
        out, later = _sb_block(q, k, v, ones_u, later, None)
        return acc + out, later

    acc, later = lax.fori_loop(0, qi, body, (acc, later))
    o_ref[...] = acc.astype(o_ref.dtype)


def _sb_prompt(q_sb, kv_sb, nb, t):
    tq = 256
    nq = t // tq
    return pl.pallas_call(
        functools.partial(_sb_prompt_kernel, tq=tq),
        out_shape=jax.ShapeDtypeStruct(q_sb.shape, BF16),
        grid=(nb, SB_HEADS, nq),
        in_specs=[pl.BlockSpec((tq, HEAD_DIM), lambda b, h, i: (b * nq + i, h)),
                  pl.BlockSpec((t, HEAD_DIM), lambda b, h, i: (b, h)),
                  pl.BlockSpec((t, HEAD_DIM), lambda b, h, i: (b, SB_HEADS + h))],
        out_specs=pl.BlockSpec((tq, HEAD_DIM), lambda b, h, i: (b * nq + i, h)),
        compiler_params=_cparams("arbitrary", "arbitrary", "arbitrary"),
        name="sb_prompt",
    )(q_sb, kv_sb, kv_sb)


def _later_ones(n):
    s = lax.broadcasted_iota(I32, (n, n), 0)
    j = lax.broadcasted_iota(I32, (n, n), 1)
    return jnp.where(j >= s, 1.0, 0.0).astype(BF16)


def _sb_sample_kernel(pt_ref, qr_ref, new_ref, *refs, n_pg, dec):
    pages = refs[:n_pg]
    o_ref = refs[n_pg]
    acc_ref, later_ref = refs[n_pg + 1:]
    j = pl.program_id(1)
    cols = SB_HEADS * dec
    rows_per_tok = 2 * SB_HEADS

    def process(k_heads, v_heads, n_keys, mask):
        z = _dot(k_heads[0], qr_ref[0, 0])
        for h in range(1, SB_HEADS):
            z = z + _dot(k_heads[h], qr_ref[0, h])
        sp = _softplus(z)
        if mask is not None:
            sp = jnp.where(mask, sp, 0.0)
        sp_hi, sp_lo = _split(sp)
        ones_l = _later_ones(n_keys)
        csum = _dot(ones_l, sp_hi) + _dot(ones_l, sp_lo)
        a = jnp.exp(z - csum - later_ref[...])
        if mask is not None:
            a = jnp.where(mask, a, 0.0)
        a = a.T.astype(BF16)
        for h in range(SB_HEADS):
            rs = slice(h * dec, (h + 1) * dec)
            acc_ref[rs, :] += _dot(a[rs], v_heads[h])
        later_ref[...] += csum[0:1, :]

    @pl.when(j == 0)
    def _():
        acc_ref[...] = jnp.zeros_like(acc_ref)
        later_ref[...] = jnp.zeros_like(later_ref)
        pad = jnp.zeros((PAGE - dec, HEAD_DIM), F32)
        new = new_ref[...]
        head = lambda c: jnp.concatenate([new[:, c * HEAD_DIM:(c + 1) * HEAD_DIM], pad], axis=0).astype(BF16)
        s = lax.broadcasted_iota(I32, (PAGE, cols), 0)
        t = lax.broadcasted_iota(I32, (PAGE, cols), 1) % dec
        process([head(h) for h in range(SB_HEADS)], [head(SB_HEADS + h) for h in range(SB_HEADS)], PAGE, s < t)

    def rows_of(c):
        return jnp.concatenate([pages[n_pg - 1 - p].at[0][pl.ds(c, PAGE, stride=rows_per_tok), :]
                                for p in range(n_pg)], axis=0).astype(BF16)

    process([rows_of(h) for h in range(SB_HEADS)], [rows_of(SB_HEADS + h) for h in range(SB_HEADS)],
            n_pg * PAGE, None)

    @pl.when(j == pl.num_programs(1) - 1)
    def _():
        for h in range(SB_HEADS):
            o_ref[:, h * HEAD_DIM:(h + 1) * HEAD_DIM] = acc_ref[h * dec:(h + 1) * dec, :].astype(o_ref.dtype)


def _sb_sample(q_rhs, kv_new, cache, page_table, dec):
    nb, n_pages = page_table.shape
    n_pg = _pick_tile(n_pages, 4)
    width = SB_HEADS * HEAD_DIM
    cols = SB_HEADS * dec
    assert cols == LANES
    page_rows = cache.shape[1]

    def page_spec(p):
        return pl.BlockSpec((1, page_rows, HEAD_DIM),
                            lambda b, j, pt: (pt[b, n_pages - 1 - (j * n_pg + p)], 0, 0))

    grid_spec = pltpu.PrefetchScalarGridSpec(
        num_scalar_prefetch=1,
        grid=(nb, n_pages // n_pg),
        in_specs=[pl.BlockSpec((1, SB_HEADS, HEAD_DIM, cols), lambda b, j, pt: (b, 0, 0, 0)),
                  pl.BlockSpec((dec, 2 * width), lambda b, j, pt: (b, 0))]
        + [page_spec(p) for p in range(n_pg)],
        out_specs=pl.BlockSpec((dec, width), lambda b, j, pt: (b, 0)),
        scratch_shapes=[pltpu.VMEM((cols, HEAD_DIM), F32), pltpu.VMEM((1, cols), F32)],
    )
    return pl.pallas_call(
        functools.partial(_sb_sample_kernel, n_pg=n_pg, dec=dec),
        out_shape=jax.ShapeDtypeStruct((nb * dec, width), BF16),
        grid_spec=grid_spec,
        compiler_params=_cparams("arbitrary", "arbitrary"),
        name="sb_sample",
    )(page_table, q_rhs, kv_new, *([cache] * n_pg))


def _compress_kernel(pt_ref, *refs, n_pg):
    pages = refs[:n_pg]
    w_refs = refs[n_pg:n_pg + 2]
    out_refs = refs[n_pg + 2:n_pg + 4]
    cpp = PAGE // CMP_STRIDE
    half = n_pg * cpp
    rows_per_tok = 4 * NSA_G
    for slot in range(2):
        acc = jnp.zeros((2 * half, 2 * CMP_HIDDEN), F32)
        for r in range(CMP_STRIDE):
            pieces = []
            for g in range(NSA_G):
                first = r * rows_per_tok + slot * NSA_G + g
                for p in range(n_pg):
                    pieces.append(pages[p].at[0][pl.ds(first, cpp, stride=CMP_STRIDE * rows_per_tok), :])
            x = jnp.concatenate(pieces, axis=0).astype(BF16)
            acc = acc + _dot(x, w_refs[slot][r])
        for g in range(NSA_G):
            out_refs[slot][0, g] = acc[g * half:(g + 1) * half]


def _compress(rows, table, w_k, w_v, n_pg):
    nb, n_pages = table.shape
    assert n_pages % n_pg == 0
    cpp = PAGE // CMP_STRIDE
    page_rows = rows.shape[1]

    def page_spec(p):
        return pl.BlockSpec((1, page_rows, HEAD_DIM), lambda b, j, pt: (pt[b, j * n_pg + p], 0, 0))

    w_spec = pl.BlockSpec((CMP_STRIDE, HEAD_DIM, 2 * CMP_HIDDEN), lambda b, j, pt: (0, 0, 0))
    out_spec = pl.BlockSpec((1, NSA_G, n_pg * cpp, 2 * CMP_HIDDEN), lambda b, j, pt: (b, 0, j, 0))
    out_sds = jax.ShapeDtypeStruct((nb, NSA_G, n_pages * cpp, 2 * CMP_HIDDEN), F32)
    grid_spec = pltpu.PrefetchScalarGridSpec(
        num_scalar_prefetch=1,
        grid=(nb, n_pages // n_pg),
        in_specs=[page_spec(p) for p in range(n_pg)] + [w_spec, w_spec],
        out_specs=[out_spec, out_spec],
    )
    return pl.pallas_call(
        functools.partial(_compress_kernel, n_pg=n_pg),
        out_shape=[out_sds, out_sds],
        grid_spec=grid_spec,
        compiler_params=_cparams("arbitrary", "arbitrary"),
        name="nsa_compress",
    )(table, *([rows] * n_pg), w_k, w_v)


def _gelu_tanh(x):
    return 0.5 * x * (1.0 + jnp.tanh(0.7978845608028654 * (x + 0.044715 * x * x * x)))


def _cmp_finish_kernel(hk_ref, hv_ref, posk_ref, posv_ref, w1k_ref, w1v_ref, w2k_ref, w2v_ref, g_ref,
                       kc_ref, vc_ref, *, nc_pad):
    row = lax.broadcasted_iota(I32, (nc_pad, CMP_HIDDEN), 0)

    def branch(h_ref, pos_ref, w1_ref, w2_ref):
        h = h_ref[0, 0]
        nxt = pltpu.roll(h[:, CMP_HIDDEN:], nc_pad - 1, 0)
        nxt = jnp.where(row == nc_pad - 1, 0.0, nxt)
        bias = _dot3(pos_ref[...], w1_ref[...])[0:1]
        hid = _gelu_tanh(h[:, :CMP_HIDDEN] + nxt + bias)
        return _dot(hid.astype(BF16), w2_ref[...].astype(BF16))

    kc_ref[0, 0] = _rms(branch(hk_ref, posk_ref, w1k_ref, w2k_ref), g_ref[...])
    vc_ref[0, 0] = branch(hv_ref, posv_ref, w1v_ref, w2v_ref)


def _cmp_finish(hk, hv, pos_k, pos_v, w1k, w1v, w2k, w2v, kc_g):
    nb, _, nc_pad, _ = hk.shape
    h_spec = pl.BlockSpec((1, 1, nc_pad, 2 * CMP_HIDDEN), lambda b, g: (b, g, 0, 0))
    o_spec = pl.BlockSpec((1, 1, nc_pad, HEAD_DIM), lambda b, g: (b, g, 0, 0))
    o_sds = jax.ShapeDtypeStruct((nb, NSA_G, nc_pad, HEAD_DIM), F32)

    def full(a):
        return pl.BlockSpec(a.shape, lambda b, g: (0,) * a.ndim)

    pos_k8 = jnp.tile(pos_k.reshape(1, -1), (8, 1))
    pos_v8 = jnp.tile(pos_v.reshape(1, -1), (8, 1))
    kc_g = kc_g.reshape(1, HEAD_DIM)
    args = (pos_k8, pos_v8, w1k, w1v, w2k, w2v, kc_g)
    return pl.pallas_call(
        functools.partial(_cmp_finish_kernel, nc_pad=nc_pad),
        out_shape=[o_sds, o_sds],
        grid=(nb, NSA_G),
        in_specs=[h_spec, h_spec] + [full(a) for a in args],
        out_specs=[o_spec, o_spec],
        compiler_params=_cparams("arbitrary", "arbitrary"),
        name="nsa_cmp_finish",
    )(hk, hv, *args)


def _cmp_select_kernel(slopes_ref, q_ref, kc_ref, vc_ref, oc_ref, sel_ref, *, tr, nc_pad, ns_pad, ns, base):
    g = pl.program_id(1)
    i = pl.program_id(2)
    q = q_ref[...]
    kc = kc_ref[0, 0].astype(BF16)
    vc = vc_ref[0, 0].astype(BF16)
    pos = base + i * tr + lax.broadcasted_iota(I32, (tr, 1), 0)
    end = lax.broadcasted_iota(I32, (1, nc_pad), 1) * CMP_STRIDE + (CMP_BLOCK - 1)
    dist = (pos - end).astype(F32)
    valid = dist >= 0.0
    score = jnp.zeros((tr, nc_pad), F32)
    for r in range(NSA_REP):
        sl = slice(r * HEAD_DIM, (r + 1) * HEAD_DIM)
        s = _dot_t(q[:, sl], kc) - slopes_ref[g, r] * dist
        p = _masked_softmax(s, valid)
        oc_ref[:, sl] = _dot(p.astype(BF16), vc)
        score = score + p
    n_id = lax.broadcasted_iota(I32, (nc_pad, ns_pad), 0)
    j_id = lax.broadcasted_iota(I32, (nc_pad, ns_pad), 1)
    group = jnp.where((n_id // SEL_RATIO) == j_id, 1.0, 0.0).astype(BF16)
    sc = _dot2_exact_rhs(score, group)
    j = lax.broadcasted_iota(I32, (tr, ns_pad), 1)
    cur = pos // SEL_BLOCK
    forced = (j == 0) | (j == cur) | (j == cur - 1)
    sc = jnp.where(j > cur, NEG_INF, jnp.where(forced, FORCE_SCORE, sc))
    sc = jnp.where(j < ns, sc, LOWEST)
    rank = jnp.zeros((tr, ns_pad), F32)
    for c in range(ns):
        other = sc[:, c:c + 1]
        ahead = (other > sc) | ((other == sc) & (c < j))
        rank = rank + jnp.where(ahead, 1.0, 0.0)
    sel_ref[0, 0] = jnp.where((rank < float(min(SEL_TOPK, ns))) & (j < ns), 1.0, 0.0)


def _cmp_select(q_n, kc, vc, slopes, nb, t, tr, ns, base):
    nc_pad = kc.shape[2]
    ns_pad = -(-ns // LANES) * LANES
    nt = t // tr
    width = NSA_REP * HEAD_DIM
    kv_spec = pl.BlockSpec((1, 1, nc_pad, HEAD_DIM), lambda b, g, i: (b, g, 0, 0))
    return pl.pallas_call(
        functools.partial(_cmp_select_kernel, tr=tr, nc_pad=nc_pad, ns_pad=ns_pad, ns=ns, base=base),
        out_shape=[jax.ShapeDtypeStruct(q_n.shape, F32),
                   jax.ShapeDtypeStruct((nb, NSA_G, t, ns_pad), F32)],
        grid=(nb, NSA_G, nt),
        in_specs=[pl.BlockSpec(memory_space=pltpu.SMEM),
                  pl.BlockSpec((tr, width), lambda b, g, i: (b * nt + i, g)), kv_spec, kv_spec],
        out_specs=[pl.BlockSpec((tr, width), lambda b, g, i: (b * nt + i, g)),
                   pl.BlockSpec((1, 1, tr, ns_pad), lambda b, g, i: (b, g, i, 0))],
        compiler_params=_cparams("arbitrary", "arbitrary", "arbitrary"),
        name="nsa_cmp_select",
    )(slopes, q_n, kc, vc)


def _block_expand(n_blocks, key_pos):
    j = lax.broadcasted_iota(I32, (n_blocks, key_pos.shape[1]), 0)
    return jnp.where(j == key_pos // SEL_BLOCK, 1.0, 0.0).astype(BF16)


def _nsa_prompt_kernel(slopes_ref, q_ref, sk_ref, sv_ref, wk_ref, wv_ref, sel_ref, oc_ref, gate_ref, o_ref,
                       *, tq, t):
    g = pl.program_id(1)
    i = pl.program_id(2)
    t0 = i * tq
    q = q_ref[...]
    sk = sk_ref[...].astype(BF16)
    sv = sv_ref[...].astype(BF16)
    span = WINDOW + tq
    wstart = pl.multiple_of(jnp.clip(t0 - WINDOW, 0, t - span), tq)
    wk = wk_ref[pl.ds(wstart, span), :].astype(BF16)
    wv = wv_ref[pl.ds(wstart, span), :].astype(BF16)
    qpos = t0 + lax.broadcasted_iota(I32, (tq, 1), 0)
    kpos = lax.broadcasted_iota(I32, (1, t), 1)
    dist = (qpos - kpos).astype(F32)
    sel = sel_ref[0, 0].astype(BF16)
    picked = _dot(sel, _block_expand(sel.shape[1], kpos))
    mask_s = (dist >= 0.0) & (picked > 0.5)
    dist_w = (qpos - (wstart + lax.broadcasted_iota(I32, (1, span), 1))).astype(F32)
    mask_w = (dist_w >= 0.0) & (dist_w < float(WINDOW))
    gates = gate_ref[0]
    for r in range(NSA_REP):
        sl = slice(r * HEAD_DIM, (r + 1) * HEAD_DIM)
        slope = slopes_ref[g, r]
        qh = q[:, sl]
        p_s = _masked_softmax(_dot_t(qh, sk) - slope * dist, mask_s)
        o_s = _dot(p_s.astype(BF16), sv)
        p_w = _masked_softmax(_dot_t(qh, wk) - slope * dist_w, mask_w)
        o_w = _dot(p_w.astype(BF16), wv)
        out = (gates[:, 3 * r:3 * r + 1] * oc_ref[:, sl] + gates[:, 3 * r + 1:3 * r + 2] * o_s
               + gates[:, 3 * r + 2:3 * r + 3] * o_w)
        o_ref[:, sl] = out.astype(o_ref.dtype)


def _nsa_prompt(q_n, nsa_kv, win_kv, sel, o_c, gates, slopes, nb, t):
    tq = 128
    assert t >= WINDOW + tq and t % tq == 0
    nt = t // tq
    width = NSA_REP * HEAD_DIM
    ns_pad = sel.shape[-1]
    q_spec = pl.BlockSpec((tq, width), lambda b, g, i: (b * nt + i, g))

    def kv_spec(slot):
        return pl.BlockSpec((t, HEAD_DIM), lambda b, g, i: (b, slot * NSA_G + g))

    return pl.pallas_call(
        functools.partial(_nsa_prompt_kernel, tq=tq, t=t),
        out_shape=jax.ShapeDtypeStruct(q_n.shape, BF16),
        grid=(nb, NSA_G, nt),
        in_specs=[pl.BlockSpec(memory_space=pltpu.SMEM), q_spec,
                  kv_spec(2), kv_spec(3), kv_spec(0), kv_spec(1),
                  pl.BlockSpec((1, 1, tq, ns_pad), lambda b, g, i: (b, g, i, 0)),
                  q_spec,
                  pl.BlockSpec((1, tq, LANES), lambda b, g, i: (g, b * nt + i, 0))],
        out_specs=q_spec,
        compiler_params=_cparams("arbitrary", "arbitrary", "arbitrary"),
        name="nsa_prompt",
    )(slopes, q_n, nsa_kv, nsa_kv, win_kv, win_kv, sel, o_c, gates)


def _nsa_sample_kernel(pt_ref, slopes_ref, q_ref, sel_ref, oc_ref, gate_ref, new_ref, wnew_ref, wcache_ref,
                       *refs, n_pg, dec, past):
    pages = refs[:n_pg]
    o_ref = refs[n_pg]
    m_ref, l_ref, acc_ref = refs[n_pg + 1:]
    j = pl.program_id(1)
    rows = NSA_REP * dec
    ns_pad = sel_ref.shape[-1]
    row = lax.broadcasted_iota(I32, (rows, 1), 0)
    qpos = past + row % dec

    @pl.when(j == 0)
    def _():
        m_ref[...] = jnp.full_like(m_ref, NEG_INF)
        l_ref[...] = jnp.zeros_like(l_ref)
        acc_ref[...] = jnp.zeros_like(acc_ref)

    def stack_heads(x, g):
        return jnp.concatenate(
            [x[:, (g * NSA_REP + r) * HEAD_DIM:(g * NSA_REP + r + 1) * HEAD_DIM] for r in range(NSA_REP)], axis=0)

    def slope_col(g):
        col = jnp.zeros((rows, 1), F32)
        for r in range(NSA_REP):
            col = jnp.where(row // dec == r, slopes_ref[g, r], col)
        return col

    def online(g, s, mask, v):
        s = jnp.where(mask, s, NEG_INF)
        m_old = m_ref[g]
        m_new = jnp.maximum(m_old, jnp.max(s, axis=-1, keepdims=True))
        alpha = jnp.exp(m_old - m_new)
        e = jnp.where(mask, jnp.exp(s - m_new), 0.0)
        l_ref[g] = alpha * l_ref[g] + jnp.sum(e, axis=-1, keepdims=True)
        acc_ref[g] = alpha * acc_ref[g] + _dot(e.astype(BF16), v)
        m_ref[g] = m_new

    def sel_rows(g):
        return jnp.concatenate([sel_ref[0, g]] * NSA_REP, axis=0).astype(BF16)

    def attend_selected(g, qg, k, v, kpos):
        dist = (qpos - kpos).astype(F32)
        picked = _dot(sel_rows(g), _block_expand(ns_pad, kpos))
        s = _dot_t(qg, k) - slope_col(g) * dist
        online(g, s, (dist >= 0.0) & (picked > 0.5), v)

    q = q_ref[...]
    n_keys = n_pg * PAGE
    kpos_pages = j * n_keys + lax.broadcasted_iota(I32, (1, n_keys), 1)
    rows_per_tok = 4 * NSA_G

    def page_rows(c):
        return jnp.concatenate([pages[p].at[0][pl.ds(c, PAGE, stride=rows_per_tok), :] for p in range(n_pg)],
                               axis=0).astype(BF16)

    for g in range(NSA_G):
        attend_selected(g, stack_heads(q, g), page_rows(2 * NSA_G + g), page_rows(3 * NSA_G + g), kpos_pages)

    @pl.when(j == pl.num_programs(1) - 1)
    def _():
        pad = jnp.zeros((PAGE - dec, HEAD_DIM), F32)
        kpos_new = past + lax.broadcasted_iota(I32, (1, PAGE), 1)
        win_rows = 2 * NSA_G
        nbuf = wcache_ref.shape[1] // win_rows
        kpos_w = jnp.concatenate([past - nbuf + lax.broadcasted_iota(I32, (1, nbuf), 1), kpos_new], axis=1)
        dist_w = (qpos - kpos_w).astype(F32)
        key_ok = jnp.concatenate([jnp.ones((1, nbuf), F32),
                                  jnp.where(lax.broadcasted_iota(I32, (1, PAGE), 1) < dec, 1.0, 0.0)], axis=1)
        mask_w = (dist_w >= 0.0) & (dist_w < float(WINDOW)) & (kpos_w >= 0) & (key_ok > 0.5)
        for g in range(NSA_G):
            qg = stack_heads(q, g)
            lane_k = (2 * NSA_G + g) * HEAD_DIM
            lane_v = (3 * NSA_G + g) * HEAD_DIM
            k_new = jnp.concatenate([new_ref[:, lane_k:lane_k + HEAD_DIM], pad], axis=0).astype(BF16)
            v_new = jnp.concatenate([new_ref[:, lane_v:lane_v + HEAD_DIM], pad], axis=0).astype(BF16)
            attend_selected(g, qg, k_new, v_new, kpos_new)
            o_s = acc_ref[g] / l_ref[g]
            lw_k = g * HEAD_DIM
            lw_v = (NSA_G + g) * HEAD_DIM
            wk = jnp.concatenate([wcache_ref.at[0][pl.ds(g, nbuf, stride=win_rows), :],
                                  wnew_ref[:, lw_k:lw_k + HEAD_DIM], pad], axis=0).astype(BF16)
            wv = jnp.concatenate([wcache_ref.at[0][pl.ds(NSA_G + g, nbuf, stride=win_rows), :],
                                  wnew_ref[:, lw_v:lw_v + HEAD_DIM], pad], axis=0).astype(BF16)
            p_w = _masked_softmax(_dot_t(qg, wk) - slope_col(g) * dist_w, mask_w)
            o_w = _dot(p_w.astype(BF16), wv)
            gates = gate_ref[g]
            for r in range(NSA_REP):
                rs = slice(r * dec, (r + 1) * dec)
                sl = slice((g * NSA_REP + r) * HEAD_DIM, (g * NSA_REP + r + 1) * HEAD_DIM)
                out = (gates[:, 3 * r:3 * r + 1] * oc_ref[:, sl] + gates[:, 3 * r + 1:3 * r + 2] * o_s[rs]
                       + gates[:, 3 * r + 2:3 * r + 3] * o_w[rs])
                o_ref[:, sl] = out.astype(o_ref.dtype)


def _nsa_sample(q_n, sel, o_c, gates, nsa_new, win_new, win_cache, cache, page_table, slopes, dec):
    nb, n_pages = page_table.shape
    n_pg = _pick_tile(n_pages, 8)
    past = n_pages * PAGE
    rows = NSA_REP * dec
    ns_pad = sel.shape[-1]
    lanes = nsa_new.shape[-1]
    wl = win_new.shape[-1]
    page_rows = cache.shape[1]

    def page_spec(p):
        return pl.BlockSpec((1, page_rows, HEAD_DIM), lambda b, j, pt: (pt[b, j * n_pg + p], 0, 0))

    tok = lambda w: pl.BlockSpec((dec, w), lambda b, j, pt: (b, 0))
    grid_spec = pltpu.PrefetchScalarGridSpec(
        num_scalar_prefetch=1,
        grid=(nb, n_pages // n_pg),
        in_specs=[pl.BlockSpec(memory_space=pltpu.SMEM),
                  tok(NSA_HEADS * HEAD_DIM),
                  pl.BlockSpec((1, NSA_G, dec, ns_pad), lambda b, j, pt: (b, 0, 0, 0)),
                  tok(NSA_HEADS * HEAD_DIM),
                  pl.BlockSpec((NSA_G, dec, LANES), lambda b, j, pt: (0, b, 0)),
                  tok(lanes), tok(wl),
                  pl.BlockSpec((1, win_cache.shape[1], HEAD_DIM), lambda b, j, pt: (b, 0, 0))]
        + [page_spec(p) for p in range(n_pg)],
        out_specs=tok(NSA_HEADS * HEAD_DIM),
        scratch_shapes=[pltpu.VMEM((NSA_G, rows, 1), F32), pltpu.VMEM((NSA_G, rows, 1), F32),
                        pltpu.VMEM((NSA_G, rows, HEAD_DIM), F32)],
    )
    return pl.pallas_call(
        functools.partial(_nsa_sample_kernel, n_pg=n_pg, dec=dec, past=past),
        out_shape=jax.ShapeDtypeStruct(q_n.shape, BF16),
        grid_spec=grid_spec,
        compiler_params=_cparams("arbitrary", "arbitrary"),
        name="nsa_sample",
    )(page_table, slopes, q_n, sel, o_c, gates, nsa_new, win_new, win_cache, *([cache] * n_pg))


def _resid_route_kernel(x_ref, mix_ref, gate_ref, g_ref, scale_ref, shift_ref, wr_hi_ref, wr_lo_ref, br_ref,
                        x1_ref, h_ref, rid_ref, rw_ref):
    x1 = x_ref[...] + _as2d(gate_ref[...]) * mix_ref[...]
    x1_ref[...] = x1
    h = _rms(x1, g_ref[...]) * (1.0 + _as2d(scale_ref[...])) + _as2d(shift_ref[...])
    h_ref[...] = h
    h_hi, h_lo = _split(h)
    logits = _dot(h_hi, wr_hi_ref[...]) + _dot(h_lo, wr_hi_ref[...]) + _dot(h_hi, wr_lo_ref[...]) + br_ref[...]
    tm = logits.shape[0]
    lane = lax.broadcasted_iota(I32, (tm, LANES), 1).astype(F32)
    far = float(LANES)

    def first_lane(cond):
        return jnp.min(jnp.where(cond, lane, far), axis=-1, keepdims=True)

    in_groups = lane < float(MOE_GROUPS)
    lg = jnp.where(in_groups, logits, LOWEST)
    g_max = jnp.max(lg, axis=-1, keepdims=True)
    g_sum = jnp.sum(jnp.where(in_groups, jnp.exp(lg - g_max), 0.0), axis=-1, keepdims=True)
    g_top = 1.0 / g_sum
    g_idx = first_lane(in_groups & (lg == g_max))
    lo = float(MOE_GROUPS) + float(EPG) * g_idx
    in_experts = (lane >= lo) & (lane < lo + float(EPG))
    le = jnp.where(in_experts, logits, LOWEST)
    e_max = jnp.max(le, axis=-1, keepdims=True)
    e_sum = jnp.sum(jnp.where(in_experts, jnp.exp(le - e_max), 0.0), axis=-1, keepdims=True)
    i1 = first_lane(in_experts & (le == e_max))
    le2 = jnp.where(lane == i1, LOWEST, le)
    e_max2 = jnp.max(le2, axis=-1, keepdims=True)
    i2 = first_lane(in_experts & (lane != i1) & (le2 == e_max2))
    p1 = 1.0 / e_sum
    p2 = jnp.exp(e_max2 - e_max) / e_sum
    w1 = p1 / (p1 + p2) * g_top
    w2 = p2 / (p1 + p2) * g_top
    first = lane == 0.0
    second = lane == 1.0
    ids = jnp.where(first, i1 - float(MOE_GROUPS), jnp.where(second, i2 - float(MOE_GROUPS), 0.0))
    rid_ref[...] = ids.astype(I32)
    rw_ref[...] = jnp.where(first, w1, jnp.where(second, w2, 0.0))


def _resid_route(x, mix, gain, mod, per_token, nb, wr_hi, wr_lo, br, tm):
    m, d = x.shape
    nt = m // nb // tm
    row = pl.BlockSpec((tm, d), lambda b, i: (b * nt + i, 0))
    lane_row = pl.BlockSpec((tm, LANES), lambda b, i: (b * nt + i, 0))
    const = lambda a: pl.BlockSpec(a.shape, lambda b, i: (0, 0))
    gain = gain.reshape(1, d)
    return pl.pallas_call(
        _resid_route_kernel,
        out_shape=[jax.ShapeDtypeStruct((m, d), F32), jax.ShapeDtypeStruct((m, d), F32),
                   jax.ShapeDtypeStruct((m, LANES), I32), jax.ShapeDtypeStruct((m, LANES), F32)],
        grid=(nb, nt),
        in_specs=[row, row, _mod_spec(per_token, tm, d, 2, nt), const(gain),
                  _mod_spec(per_token, tm, d, 4, nt), _mod_spec(per_token, tm, d, 3, nt),
                  const(wr_hi), const(wr_lo), const(br)],
        out_specs=[row, row, lane_row, lane_row],
        compiler_params=_cparams("arbitrary", "arbitrary"),
        name="resid_route",
    )(x, mix, mod, gain, mod, mod, wr_hi, wr_lo, br)


def _moe_kernel(texp_ref, nused_ref, src_ref, dst_ref, h_hbm, roww_ref, wg_ref, wu_ref, wd_ref, y_hbm,
                xbuf, xbf, acc, sem_in, sem_out, *, tm):
    i = pl.program_id(0)
    f = pl.program_id(1)

    def gather_copy(r, src_row):
        return pltpu.make_async_copy(h_hbm.at[pl.ds(src_row, 1)], xbuf.at[pl.ds(r, 1)], sem_in)

    def scatter_copy(r, dst_row):
        return pltpu.make_async_copy(xbuf.at[pl.ds(r, 1)], y_hbm.at[pl.ds(dst_row, 1)], sem_out)

    @pl.when(i < nused_ref[0])
    def _():
        @pl.when(f == 0)
        def _():
            def start(r, c):
                gather_copy(r, src_ref[i * tm + r]).start()
                return c

            def wait(r, c):
                gather_copy(r, 0).wait()
                return c

            lax.fori_loop(0, tm, start, 0)
            lax.fori_loop(0, tm, wait, 0)
            xbf[...] = xbuf[...].astype(BF16)

        x = xbf[...]
        gt = _dot(x, wg_ref[0].astype(BF16))
        up = _dot(x, wu_ref[0].astype(BF16))
        hid = gt * _sigmoid(gt) * up
        part = _dot(hid.astype(BF16), wd_ref[0].astype(BF16))

        @pl.when(f == 0)
        def _():
            acc[...] = part

        @pl.when(f == pl.num_programs(1) - 1)
        def _():
            xbuf[...] = (acc[...] + part) * roww_ref[...]

            def start(r, c):
                dst = dst_ref[i * tm + r]

                @pl.when(dst >= 0)
                def _():
                    scatter_copy(r, dst).start()

                return c

            def wait(r, c):
                @pl.when(dst_ref[i * tm + r] >= 0)
                def _():
                    scatter_copy(r, 0).wait()

                return c

            lax.fori_loop(0, tm, start, 0)
            lax.fori_loop(0, tm, wait, 0)


def _moe(h_all, route_id, route_w, w_gate, w_up, w_down):
    n, d = h_all.shape
    tm = 256
    n_ff = 2
    ff = EXPERT_FF // n_ff
    a = 2 * n
    t_max = (a + N_EXPERTS * (tm - 1)) // tm
    e_flat = route_id.reshape(-1)
    order = jnp.argsort(e_flat, stable=True).astype(I32)
    counts = jnp.bincount(e_flat, length=N_EXPERTS).astype(I32)
    tiles_per = (counts + tm - 1) // tm
    tiles_end = jnp.cumsum(tiles_per)
    n_used = tiles_end[-1]
    e_sorted = e_flat[order]
    rank = jnp.arange(a, dtype=I32) - (jnp.cumsum(counts) - counts)[e_sorted]
    dest = (tiles_end - tiles_per)[e_sorted] * tm + rank
    tok = order // 2
    src_row = jnp.zeros((t_max * tm,), I32).at[dest].set(tok)
    dst_row = jnp.full((t_max * tm,), -1, I32).at[dest].set((order % 2) * n + tok)
    row_w = jnp.zeros((t_max * tm,), F32).at[dest].set(route_w.reshape(-1)[order]).reshape(-1, 1)
    tile = jnp.arange(t_max, dtype=I32)
    tile_e = jnp.searchsorted(tiles_end, jnp.minimum(tile, n_used - 1), side="right").astype(I32)
    tile_e = jnp.minimum(tile_e, N_EXPERTS - 1)

    grid_spec = pltpu.PrefetchScalarGridSpec(
        num_scalar_prefetch=4,
        grid=(t_max, n_ff),
        in_specs=[pl.BlockSpec(memory_space=pl.ANY),
                  pl.BlockSpec((tm, 1), lambda i, f, te, nu, sr, ds: (i, 0)),
                  pl.BlockSpec((1, d, ff), lambda i, f, te, nu, sr, ds: (te[i], 0, f)),
                  pl.BlockSpec((1, d, ff), lambda i, f, te, nu, sr, ds: (te[i], 0, f)),
                  pl.BlockSpec((1, ff, d), lambda i, f, te, nu, sr, ds: (te[i], f, 0))],
        out_specs=pl.BlockSpec(memory_space=pl.ANY),
        scratch_shapes=[pltpu.VMEM((tm, d), F32), pltpu.VMEM((tm, d), BF16), pltpu.VMEM((tm, d), F32),
                        pltpu.SemaphoreType.DMA(()), pltpu.SemaphoreType.DMA(())],
    )
    return pl.pallas_call(
        functools.partial(_moe_kernel, tm=tm),
        out_shape=jax.ShapeDtypeStruct((a, d), F32),
        grid_spec=grid_spec,
        compiler_params=pltpu.CompilerParams(dimension_semantics=("arbitrary", "arbitrary"),
                                             vmem_limit_bytes=VMEM_LIMIT, has_side_effects=True),
        name="moe",
    )(tile_e, n_used.reshape(1).astype(I32), src_row, dst_row, h_all, row_w, w_gate, w_up, w_down)


def _final_kernel(x_ref, gate_ref, ya_ref, yb_ref, o_ref):
    o_ref[...] = x_ref[...] + _as2d(gate_ref[...]) * (ya_ref[...] + yb_ref[...])


def _final(x1, mod, per_token, nb, y, row_off, n_all, tm):
    m, d = x1.shape
    nt = m // nb // tm
    off_a = row_off // tm
    off_b = (n_all + row_off) // tm
    row = pl.BlockSpec((tm, d), lambda b, i: (b * nt + i, 0))
    return pl.pallas_call(
        _final_kernel,
        out_shape=jax.ShapeDtypeStruct((m, d), F32),
        grid=(nb, nt),
        in_specs=[row, _mod_spec(per_token, tm, d, 5, nt),
                  pl.BlockSpec((tm, d), lambda b, i: (off_a + b * nt + i, 0)),
                  pl.BlockSpec((tm, d), lambda b, i: (off_b + b * nt + i, 0))],
        out_specs=row,
        compiler_params=_cparams("arbitrary", "arbitrary"),
        name="final_residual",
    )(x1, mod, y, y)


def _pick_tile(m, pref):
    t = pref
    while m % t:
        t //= 2
    return t


def _project(h, w_in, gains, d):
    m = h.shape[0]
    tm = _pick_tile(m, 1024)
    tn = 512
    sbw = SB_HEADS * HEAD_DIM
    nw = NSA_HEADS * HEAD_DIM
    kvw = NSA_G * HEAD_DIM
    o_q, o_kv, o_qn, o_nkv, o_win, o_gn = 0, sbw, 3 * sbw, 3 * sbw + nw, 3 * sbw + nw + 4 * kvw, 3 * sbw + nw + 6 * kvw
    mm = functools.partial(_matmul, [h], [w_in], tm=tm, tn=tn)
    q_sb = mm(o_q, sbw, [], _ep_scaled, BF16, name="proj_q_sb")
    kv_sb = mm(o_kv, 2 * sbw, [], _ep_plain, F32, name="proj_kv_sb")

    def normed(off, width, gain_cols, flag_cols, post, dtype, name):
        return mm(off, width, [("col", gain_cols.reshape(1, width)), ("col", flag_cols.reshape(1, width))],
                  functools.partial(_ep_groupnorm, post_scale=post), dtype, name=name)

    ones = lambda k: jnp.ones((k * HEAD_DIM,), F32)
    zeros = lambda k: jnp.zeros((k * HEAD_DIM,), F32)
    q_n = normed(o_qn, nw, jnp.tile(gains["q"], NSA_HEADS), ones(NSA_HEADS), SCALE, BF16, "proj_q_nsa")
    nsa_kv = normed(o_nkv, 4 * kvw,
                    jnp.concatenate([ones(2 * NSA_G), jnp.tile(gains["ks"], NSA_G), ones(NSA_G)]),
                    jnp.concatenate([zeros(2 * NSA_G), ones(NSA_G), zeros(NSA_G)]), 1.0, F32, "proj_nsa_kv")
    win_kv = normed(o_win, 2 * kvw, jnp.concatenate([jnp.tile(gains["kw"], NSA_G), ones(NSA_G)]),
                    jnp.concatenate([ones(NSA_G), zeros(NSA_G)]), 1.0, F32, "proj_win_kv")
    g_n = mm(o_gn, tn, [], _ep_sigmoid, F32, name="proj_g_nsa")[:, :3 * NSA_HEADS]
    w_gates = w_in[:, o_gn + 3 * NSA_HEADS:]
    gate_sb = _matmul([h], [w_gates], 0, d, [], _ep_sigmoid, BF16, tm, tn, "proj_gate_sb")
    gate_n = _matmul([h], [w_gates], d, d, [], _ep_sigmoid, BF16, tm, tn, "proj_gate_nsa")
    gates = jnp.pad(g_n.reshape(m, NSA_G, 3 * NSA_REP).transpose(1, 0, 2),
                    ((0, 0), (0, 0), (0, LANES - 3 * NSA_REP)))
    return q_sb, kv_sb, q_n, nsa_kv, win_kv, gates, gate_sb, gate_n


def _compress_weights(w1):
    half = CMP_STRIDE * HEAD_DIM
    a = w1[:half].reshape(CMP_STRIDE, HEAD_DIM, CMP_HIDDEN)
    b = w1[half:].reshape(CMP_STRIDE, HEAD_DIM, CMP_HIDDEN)
    return jnp.concatenate([a, b], axis=-1).astype(BF16)


def _merge_out(o_sb, o_n, gate_sb, gate_n, w):
    m = o_sb.shape[0]
    d = w["w_out"].shape[1]
    tm = _pick_tile(m, 1024)
    mixed = _matmul([o_sb, o_n], [w["w_br_sb"], w["w_br_nsa"]], 0, d, [("tile", gate_sb), ("tile", gate_n)],
                    _ep_gated_pair, BF16, tm, 512, "merge")
    return _matmul([mixed], [w["w_out"]], 0, d, [], _ep_plain, F32, tm, 512, "out_proj")


def _alibi_slopes():
    h = jnp.arange(1, NSA_HEADS + 1, dtype=F32)
    return jnp.exp2(-8.0 * h / NSA_HEADS).reshape(NSA_G, NSA_REP)


def _layer(xp, xs, c_all, sb_cache, nsa_cache, win_cache, page_table, w, nb, t, nbs, dec):
    d = xp.shape[1]
    n_p, n_s = xp.shape[0], xs.shape[0]
    slopes = _alibi_slopes()
    mod = _adaln(c_all, w["w_ada"], w["b_ada"])
    mod_p = mod[:nb].reshape(nb, 1, 6 * d)
    mod_s = jnp.repeat(mod[nb:nb + nbs], dec, axis=0)
    gains = {"q": w["nsa_q_g"], "ks": w["nsa_ks_g"], "kw": w["nsa_kw_g"]}
    w1k, w1v = _compress_weights(w["cmp_w1_k"]), _compress_weights(w["cmp_w1_v"])
    cpp = PAGE // CMP_STRIDE

    tm_p = _pick_tile(t, 256)
    h_p = _modulate(xp, w["norm1_g"], mod_p, False, nb, 1, 0, tm_p)
    q_sb, kv_sb_p, q_n, nsa_kv_p, win_kv_p, gates, gate_sb, gate_n = _project(h_p, w["w_in"], gains, d)
    o_sb = _sb_prompt(q_sb, kv_sb_p, nb, t)
    ident = jnp.arange(nb * (t // PAGE), dtype=I32).reshape(nb, t // PAGE)
    as_pages = lambda a, n: a.reshape(n, -1, 4 * NSA_G, HEAD_DIM).reshape(n, -1, HEAD_DIM)
    hk, hv = _compress(as_pages(nsa_kv_p, nb * t // PAGE), ident, w1k, w1v, _pick_tile(t // PAGE, 8))
    kc, vc = _cmp_finish(hk, hv, w["cmp_pos_k"], w["cmp_pos_v"], w["cmp_w1_k"], w["cmp_w1_v"],
                         w["cmp_w2_k"], w["cmp_w2_v"], w["nsa_kc_g"])
    o_c, sel = _cmp_select(q_n, kc, vc, slopes, nb, t, 128, t // SEL_BLOCK, 0)
    o_n = _nsa_prompt(q_n, nsa_kv_p, win_kv_p, sel, o_c, gates, slopes, nb, t)
    mix_p = _merge_out(o_sb, o_n, gate_sb, gate_n, w)

    n_pages = page_table.shape[1]
    past = n_pages * PAGE
    h_s = _modulate(xs, w["norm1_g"], mod_s, True, 1, 1, 0, n_s)
    q_sb, kv_sb_s, q_n, nsa_kv_s, win_kv_s, gates, gate_sb, gate_n = _project(h_s, w["w_in"], gains, d)
    eye = jnp.eye(SB_HEADS, dtype=BF16)
    q_rhs = (q_sb.reshape(nbs, dec, SB_HEADS, HEAD_DIM).transpose(0, 2, 3, 1)[:, :, :, None, :]
             * eye[None, :, None, :, None]).reshape(nbs, SB_HEADS, HEAD_DIM, SB_HEADS * dec)
    o_sb = _sb_sample(q_rhs, kv_sb_s, sb_cache.reshape(sb_cache.shape[0], -1, HEAD_DIM), page_table, dec)
    nsa_pages = nsa_cache.reshape(nsa_cache.shape[0], -1, HEAD_DIM)
    hk, hv = _compress(nsa_pages, page_table, w1k, w1v, _pick_tile(n_pages, 8))
    tail = jnp.pad(as_pages(nsa_kv_s, nbs), ((0, 0), (0, (PAGE - dec) * 4 * NSA_G), (0, 0)))
    hk_t, hv_t = _compress(tail, jnp.arange(nbs, dtype=I32).reshape(nbs, 1), w1k, w1v, 1)
    n_tot = past + dec + (-(past + dec)) % SEL_BLOCK
    nc = n_tot // CMP_STRIDE
    nc_pad = -(-nc // LANES) * LANES
    n_tail = nc - n_pages * cpp
    assert 0 <= n_tail <= cpp

    def with_tail(h_past, h_tail):
        return jnp.pad(jnp.concatenate([h_past, h_tail[:, :, :n_tail]], axis=2),
                       ((0, 0), (0, 0), (0, nc_pad - nc), (0, 0)))

    kc, vc = _cmp_finish(with_tail(hk, hk_t), with_tail(hv, hv_t), w["cmp_pos_k"], w["cmp_pos_v"],
                         w["cmp_w1_k"], w["cmp_w1_v"], w["cmp_w2_k"], w["cmp_w2_v"], w["nsa_kc_g"])
    o_c, sel = _cmp_select(q_n, kc, vc, slopes, nbs, dec, dec, n_tot // SEL_BLOCK, past)
    o_n = _nsa_sample(q_n, sel, o_c, gates, nsa_kv_s, win_kv_s, win_cache.reshape(nbs, -1, HEAD_DIM),
                      nsa_pages, page_table, slopes, dec)
    mix_s = _merge_out(o_sb, o_n, gate_sb, gate_n, w)

    wr = jnp.pad(jnp.concatenate([w["w_route_group"], w["w_route_expert"]], axis=1),
                 ((0, 0), (0, LANES - MOE_GROUPS - N_EXPERTS)))
    br = jnp.pad(jnp.concatenate([w["b_route_group"], w["b_route_expert"]]),
                 (0, LANES - MOE_GROUPS - N_EXPERTS)).reshape(1, LANES)
    wr_hi, wr_lo = _split(wr)
    tm_r = _pick_tile(t, 128)
    x1_p, h2_p, rid_p, rw_p = _resid_route(xp, mix_p, w["norm2_g"], mod_p, False, nb, wr_hi, wr_lo, br, tm_r)
    x1_s, h2_s, rid_s, rw_s = _resid_route(xs, mix_s, w["norm2_g"], mod_s, True, 1, wr_hi, wr_lo, br,
                                           _pick_tile(n_s, 128))
    h2 = jnp.concatenate([h2_p, h2_s], axis=0)
    rid = jnp.concatenate([rid_p[:, :2], rid_s[:, :2]], axis=0)
    rw = jnp.concatenate([rw_p[:, :2], rw_s[:, :2]], axis=0)
    y = _moe(h2, rid, rw, w["w_exp_gate"], w["w_exp_up"], w["w_exp_down"])
    n_all = n_p + n_s
    tm_f = _pick_tile(n_s, _pick_tile(t, 256))
    y_p = _final(x1_p, mod_p, False, nb, y, 0, n_all, tm_f)
    y_s = _final(x1_s, mod_s, True, 1, y, n_p, n_all, tm_f)

    nbuf = win_cache.shape[1]
    win_state_p = win_kv_p.reshape(nb, t, -1)[:, t - min(WINDOW, t):]
    win_state_s = jnp.concatenate([win_cache.reshape(nbs, nbuf, -1), win_kv_s.reshape(nbs, dec, -1)], axis=1)[:, dec:]
    state = (kv_sb_p, nsa_kv_p, win_state_p, kv_sb_s, nsa_kv_s, win_state_s)
    return y_p, y_s, state


def kernel(x_prompt, x_sample, c_prompt, c_sample, cache_sb_kv, cache_nsa_kv, cache_win_kv, page_table, norm1_g, norm2_g, w_ada, b_ada, w_in, nsa_q_g, nsa_kc_g, nsa_ks_g, nsa_kw_g, cmp_pos_k, cmp_pos_v, cmp_w1_k, cmp_w2_k, cmp_w1_v, cmp_w2_v, w_br_sb, w_br_nsa, w_out, w_route_group, b_route_group, w_route_expert, b_route_expert, w_exp_gate, w_exp_up, w_exp_down):
    nb, t, d = x_prompt.shape
    nbs, dec, _ = x_sample.shape
    depth = w_in.shape[0]
    weights = dict(norm1_g=norm1_g, norm2_g=norm2_g, w_ada=w_ada, b_ada=b_ada, w_in=w_in, nsa_q_g=nsa_q_g,
                   nsa_kc_g=nsa_kc_g, nsa_ks_g=nsa_ks_g, nsa_kw_g=nsa_kw_g, cmp_pos_k=cmp_pos_k,
                   cmp_pos_v=cmp_pos_v, cmp_w1_k=cmp_w1_k, cmp_w2_k=cmp_w2_k, cmp_w1_v=cmp_w1_v,
                   cmp_w2_v=cmp_w2_v, w_br_sb=w_br_sb, w_br_nsa=w_br_nsa, w_out=w_out,
                   w_route_group=w_route_group, b_route_group=b_route_group, w_route_expert=w_route_expert,
                   b_route_expert=b_route_expert, w_exp_gate=w_exp_gate, w_exp_up=w_exp_up, w_exp_down=w_exp_down)
    n_c = nb + nbs
    c_all = jnp.pad(jnp.concatenate([c_prompt, c_sample], axis=0), ((0, (-n_c) % 8), (0, 0)))
    xp = x_prompt.reshape(nb * t, d)
    xs = x_sample.reshape(nbs * dec, d)
    states = []
    for layer in range(depth):
        w = {k: v[layer] for k, v in weights.items()}
        xp, xs, st = _layer(xp, xs, c_all, cache_sb_kv[layer], cache_nsa_kv[layer], cache_win_kv[layer],
                            page_table, w, nb, t, nbs, dec)
        states.append(st)
    g = NSA_G
    stack = lambda k, shape: jnp.stack([s[k].reshape(shape) for s in states])
    return (xp.reshape(nb, t, d), xs.reshape(nbs, dec, d),
            stack(0, (nb, t, 2, SB_HEADS, HEAD_DIM)),
            stack(1, (nb, t, 4, g, HEAD_DIM)),
            stack(2, (nb, min(WINDOW, t), 2, g, HEAD_DIM)),
            stack(3, (nbs, dec, 2, SB_HEADS, HEAD_DIM)),
            stack(4, (nbs, dec, 4, g, HEAD_DIM)),
            stack(5, (nbs, cache_win_kv.shape[2], 2, g, HEAD_DIM)))
```

```python
import functools

import jax
import jax.numpy as jnp
from jax import lax
from jax.experimental import pallas as pl
from jax.experimental.pallas import tpu as pltpu

F32 = jnp.float32
BF16 = jnp.bfloat16
I32 = jnp.int32

HEAD_DIM = 128
SB_HEADS = 16
NSA_HEADS = 16
NSA_G = 2
NSA_REP = NSA_HEADS // NSA_G
CMP_STRIDE = 16
CMP_BLOCK = 2 * CMP_STRIDE
CMP_HIDDEN = 256
SEL_BLOCK = 64
SEL_RATIO = SEL_BLOCK // CMP_STRIDE
SEL_TOPK = 16
WINDOW = 512
PAGE = 128
MOE_GROUPS = 4
EPG = 8
N_EXPERTS = MOE_GROUPS * EPG
EXPERT_FF = 512
NORM_EPS = 1e-6
NEG_INF = -1e30
FORCE_SCORE = 1e4
LOWEST = -3e38
SCALE = HEAD_DIM ** -0.5
LANES = 128
SUBLANES = 8
VMEM_LIMIT = 48 * 1024 * 1024


def _cparams(*sem):
    return pltpu.CompilerParams(dimension_semantics=sem, vmem_limit_bytes=VMEM_LIMIT)


def _dot(a, b):
    return jnp.dot(a, b, preferred_element_type=F32)


def _dot_t(a, b):
    return lax.dot_general(a, b, (((1,), (1,)), ((), ())), preferred_element_type=F32)


def _split(x):
    hi = x.astype(BF16)
    lo = (x - hi.astype(F32)).astype(BF16)
    return hi, lo


def _dot3(a, b):
    a_hi, a_lo = _split(a)
    b_hi, b_lo = _split(b)
    return _dot(a_hi, b_hi) + _dot(a_lo, b_hi) + _dot(a_hi, b_lo)


def _dot2_exact_rhs(a, b_bf16):
    a_hi, a_lo = _split(a)
    return _dot(a_hi, b_bf16) + _dot(a_lo, b_bf16)


def _sigmoid(x):
    return 1.0 / (1.0 + jnp.exp(-x))


def _softplus(x):
    return jnp.maximum(x, 0.0) + jnp.log(1.0 + jnp.exp(-jnp.abs(x)))


def _rms(x, gain):
    return x * lax.rsqrt(jnp.mean(x * x, axis=-1, keepdims=True) + NORM_EPS) * gain


def _masked_softmax(s, mask):
    s = jnp.where(mask, s, NEG_INF)
    m = jnp.max(s, axis=-1, keepdims=True)
    e = jnp.where(mask, jnp.exp(s - m), 0.0)
    l = jnp.sum(e, axis=-1, keepdims=True)
    return e / jnp.where(l > 0.0, l, 1.0)


def _as2d(v):
    return v.reshape(v.shape[-2], v.shape[-1])


def _adaln_kernel(c_ref, w_ref, b_ref, o_ref):
    c = c_ref[...]
    o_ref[...] = _dot3(c * _sigmoid(c), w_ref[...]) + b_ref[...]


def _adaln(c_pad, w_ada, b_ada):
    m, d = c_pad.shape
    n = w_ada.shape[1]
    tn = 512
    return pl.pallas_call(
        _adaln_kernel,
        out_shape=jax.ShapeDtypeStruct((m, n), F32),
        grid=(n // tn,),
        in_specs=[pl.BlockSpec((m, d), lambda j: (0, 0)),
                  pl.BlockSpec((d, tn), lambda j: (0, j)),
                  pl.BlockSpec((1, tn), lambda j: (0, j))],
        out_specs=pl.BlockSpec((m, tn), lambda j: (0, j)),
        compiler_params=_cparams("arbitrary"),
        name="adaln",
    )(c_pad, w_ada, b_ada.reshape(1, n))


def _modulate_kernel(x_ref, g_ref, scale_ref, shift_ref, h_ref):
    h = _rms(x_ref[...], g_ref[...]) * (1.0 + _as2d(scale_ref[...])) + _as2d(shift_ref[...])
    h_ref[...] = h.astype(h_ref.dtype)


def _mod_spec(per_token, tm, d, col, nt):
    if per_token:
        return pl.BlockSpec((tm, d), lambda b, i: (b * nt + i, col))
    return pl.BlockSpec((1, 1, d), lambda b, i: (b, 0, col))


def _modulate(x, gain, mod, per_token, nb, scale_col, shift_col, tm):
    m, d = x.shape
    nt = m // nb // tm
    row = pl.BlockSpec((tm, d), lambda b, i: (b * nt + i, 0))
    return pl.pallas_call(
        _modulate_kernel,
        out_shape=jax.ShapeDtypeStruct((m, d), BF16),
        grid=(nb, nt),
        in_specs=[row, pl.BlockSpec((1, d), lambda b, i: (0, 0)),
                  _mod_spec(per_token, tm, d, scale_col, nt), _mod_spec(per_token, tm, d, shift_col, nt)],
        out_specs=row,
        compiler_params=_cparams("arbitrary", "arbitrary"),
        name="modulate",
    )(x, gain.reshape(1, d), mod, mod)


def _mm_kernel(*refs, n_pairs, n_extra, epilogue):
    xs = refs[:n_pairs]
    ws = refs[n_pairs:2 * n_pairs]
    extras = refs[2 * n_pairs:2 * n_pairs + n_extra]
    o_ref = refs[2 * n_pairs + n_extra]
    wbfs = refs[2 * n_pairs + n_extra + 1:]

    @pl.when(pl.program_id(1) == 0)
    def _():
        for w, wb in zip(ws, wbfs):
            wb[...] = w[...].astype(BF16)

    accs = [_dot(x[...], wb[...]) for x, wb in zip(xs, wbfs)]
    o_ref[...] = epilogue(accs, [e[...] for e in extras]).astype(o_ref.dtype)


def _matmul(xs, ws, col_off, n_cols, extras, epilogue, out_dtype, tm, tn, name):
    m = xs[0].shape[0]
    assert m % tm == 0 and n_cols % tn == 0 and col_off % tn == 0
    joff = col_off // tn
    in_specs = [pl.BlockSpec((tm, x.shape[1]), lambda j, i: (i, 0)) for x in xs]
    in_specs += [pl.BlockSpec((w.shape[0], tn), lambda j, i: (0, joff + j)) for w in ws]
    for kind, arr in extras:
        if kind == "col":
            in_specs.append(pl.BlockSpec((1, tn), lambda j, i: (0, j)))
        else:
            in_specs.append(pl.BlockSpec((tm, tn), lambda j, i: (i, j)))
    return pl.pallas_call(
        functools.partial(_mm_kernel, n_pairs=len(xs), n_extra=len(extras), epilogue=epilogue),
        out_shape=jax.ShapeDtypeStruct((m, n_cols), out_dtype),
        grid=(n_cols // tn, m // tm),
        in_specs=in_specs,
        out_specs=pl.BlockSpec((tm, tn), lambda j, i: (i, j)),
        scratch_shapes=[pltpu.VMEM((w.shape[0], tn), BF16) for w in ws],
        compiler_params=_cparams("arbitrary", "arbitrary"),
        name=name,
    )(*xs, *ws, *[a for _, a in extras])


def _ep_plain(accs, extras):
    return accs[0]


def _ep_scaled(accs, extras):
    return accs[0] * SCALE


def _ep_sigmoid(accs, extras):
    return _sigmoid(accs[0])


def _ep_gated_pair(accs, extras):
    return extras[0].astype(F32) * accs[0] + extras[1].astype(F32) * accs[1]


def _ep_groupnorm(accs, extras, post_scale):
    acc = accs[0]
    gain, flag = extras
    outs = []
    for c in range(acc.shape[1] // LANES):
        sl = slice(c * LANES, (c + 1) * LANES)
        x = acc[:, sl]
        outs.append(jnp.where(flag[:, sl] > 0.0, _rms(x, gain[:, sl]), x) * post_scale)
    return jnp.concatenate(outs, axis=1)


def _upper_ones(n):
    j = lax.broadcasted_iota(I32, (n, n), 0)
    s = lax.broadcasted_iota(I32, (n, n), 1)
    return jnp.where(j >= s, 1.0, 0.0).astype(BF16)


def _sb_block(q, k, v, ones_u, later, mask):
    z = _dot_t(q, k)
    sp = _softplus(z)
    if mask is not None:
        sp = jnp.where(mask, sp, 0.0)
    csum = _dot(sp.astype(BF16), ones_u)
    a = jnp.exp(z - csum - later)
    if mask is not None:
        a = jnp.where(mask, a, 0.0)
    return _dot(a.astype(BF16), v), later + csum[:, 0:1]


SB_HEADS_PER_STEP = 2


def _sb_prompt_kernel(q_ref, k_ref, v_ref, o_ref, *, tq):
    qi = pl.program_id(2)
    ones_u = _upper_ones(tq)
    row = lax.broadcasted_iota(I32, (tq, tq), 0)
    col = lax.broadcasted_iota(I32, (tq, tq), 1)
    heads = [slice(h * HEAD_DIM, (h + 1) * HEAD_DIM) for h in range(SB_HEADS_PER_STEP)]
    qs = [q_ref[:, sl] for sl in heads]

    def kv(kj, sl):
        start = pl.multiple_of(kj * tq, tq)
        return k_ref[pl.ds(start, tq), sl].astype(BF16), v_ref[pl.ds(start, tq), sl].astype(BF16)

    carry = []
    for q, sl in zip(qs, heads):
        k, v = kv(qi, sl)
        carry.extend(_sb_block(q, k, v, ones_u, jnp.zeros((tq, 1), F32), col < row))

    def body(step, carry):
        new = []
        for n, (q, sl) in enumerate(zip(qs, heads)):
            k, v = kv(qi - 1 - step, sl)
            out, later = _sb_block(q, k, v, ones_u, carry[2 * n + 1], None)
            new.extend((carry[2 * n] + out, later))
        return tuple(new)

    carry = lax.fori_loop(0, qi, body, tuple(carry))
    for n, sl in enumerate(heads):
        o_ref[:, sl] = carry[2 * n].astype(o_ref.dtype)


def _sb_prompt(q_sb, kv_sb, nb, t):
    tq = 256
    nq = t // tq
    width = SB_HEADS_PER_STEP * HEAD_DIM
    n_hp = SB_HEADS // SB_HEADS_PER_STEP
    return pl.pallas_call(
        functools.partial(_sb_prompt_kernel, tq=tq),
        out_shape=jax.ShapeDtypeStruct(q_sb.shape, BF16),
        grid=(nb, n_hp, nq),
        in_specs=[pl.BlockSpec((tq, width), lambda b, h, i: (b * nq + i, h)),
                  pl.BlockSpec((t, width), lambda b, h, i: (b, h)),
                  pl.BlockSpec((t, width), lambda b, h, i: (b, n_hp + h))],
        out_specs=pl.BlockSpec((tq, width), lambda b, h, i: (b * nq + i, h)),
        compiler_params=_cparams("arbitrary", "arbitrary", "arbitrary"),
        name="sb_prompt",
    )(q_sb, kv_sb, kv_sb)


def _later_ones(n):
    s = lax.broadcasted_iota(I32, (n, n), 0)
    j = lax.broadcasted_iota(I32, (n, n), 1)
    return jnp.where(j >= s, 1.0, 0.0).astype(BF16)


def _sb_sample_kernel(pt_ref, qr_ref, new_ref, *refs, n_pg, dec):
    pages = refs[:n_pg]
    o_ref = refs[n_pg]
    acc_ref, later_ref, rows_ref = refs[n_pg + 1:]
    j = pl.program_id(1)
    cols = SB_HEADS * dec
    rows_per_tok = 2 * SB_HEADS
    n_tok = n_pg * PAGE
    pitch = n_tok + SUBLANES

    def process(k_heads, v_heads, n_keys, mask):
        z = _dot(k_heads[0], qr_ref[0, 0])
        for h in range(1, SB_HEADS):
            z = z + _dot(k_heads[h], qr_ref[0, h])
        sp = _softplus(z)
        if mask is not None:
            sp = jnp.where(mask, sp, 0.0)
        sp_hi, sp_lo = _split(sp)
        ones_l = _later_ones(n_keys)
        csum = _dot(ones_l, sp_hi) + _dot(ones_l, sp_lo)
        a = jnp.exp(z - csum - later_ref[...])
        if mask is not None:
            a = jnp.where(mask, a, 0.0)
        a = a.T.astype(BF16)
        for h in range(SB_HEADS):
            rs = slice(h * dec, (h + 1) * dec)
            acc_ref[rs, :] += _dot(a[rs], v_heads[h])
        later_ref[...] += csum[0:1, :]

    @pl.when(j == 0)
    def _():
        acc_ref[...] = jnp.zeros_like(acc_ref)
        later_ref[...] = jnp.zeros_like(later_ref)
        pad = jnp.zeros((PAGE - dec, HEAD_DIM), F32)
        new = new_ref[...]
        head = lambda c: jnp.concatenate([new[:, c * HEAD_DIM:(c + 1) * HEAD_DIM], pad], axis=0).astype(BF16)
        s = lax.broadcasted_iota(I32, (PAGE, cols), 0)
        t = lax.broadcasted_iota(I32, (PAGE, cols), 1) % dec
        process([head(h) for h in range(SB_HEADS)], [head(SB_HEADS + h) for h in range(SB_HEADS)], PAGE, s < t)

    for p in range(n_pg):
        page = pages[n_pg - 1 - p].at[0]
        for tok in range(PAGE):
            for c0 in range(0, rows_per_tok, SUBLANES):
                rows_ref[pl.ds(c0 * pitch + p * PAGE + tok, SUBLANES, stride=pitch), :] = (
                    page[pl.ds(tok * rows_per_tok + c0, SUBLANES), :])

    def rows_of(c):
        return rows_ref[pl.ds(c * pitch, n_tok), :].astype(BF16)

    process([rows_of(h) for h in range(SB_HEADS)], [rows_of(SB_HEADS + h) for h in range(SB_HEADS)],
            n_tok, None)

    @pl.when(j == pl.num_programs(1) - 1)
    def _():
        for h in range(SB_HEADS):
            o_ref[:, h * HEAD_DIM:(h + 1) * HEAD_DIM] = acc_ref[h * dec:(h + 1) * dec, :].astype(o_ref.dtype)


def _sb_sample(q_rhs, kv_new, cache, page_table, dec):
    nb, n_pages = page_table.shape
    n_pg = _pick_tile(n_pages, 4)
    width = SB_HEADS * HEAD_DIM
    cols = SB_HEADS * dec
    assert cols == LANES
    page_rows = cache.shape[1]

    def page_spec(p):
        return pl.BlockSpec((1, page_rows, HEAD_DIM),
                            lambda b, j, pt: (pt[b, n_pages - 1 - (j * n_pg + p)], 0, 0))

    grid_spec = pltpu.PrefetchScalarGridSpec(
        num_scalar_prefetch=1,
        grid=(nb, n_pages // n_pg),
        in_specs=[pl.BlockSpec((1, SB_HEADS, HEAD_DIM, cols), lambda b, j, pt: (b, 0, 0, 0)),
                  pl.BlockSpec((dec, 2 * width), lambda b, j, pt: (b, 0))]
        + [page_spec(p) for p in range(n_pg)],
        out_specs=pl.BlockSpec((dec, width), lambda b, j, pt: (b, 0)),
        scratch_shapes=[pltpu.VMEM((cols, HEAD_DIM), F32), pltpu.VMEM((1, cols), F32),
                        pltpu.VMEM((2 * SB_HEADS * (n_pg * PAGE + SUBLANES), HEAD_DIM), F32)],
    )
    return pl.pallas_call(
        functools.partial(_sb_sample_kernel, n_pg=n_pg, dec=dec),
        out_shape=jax.ShapeDtypeStruct((nb * dec, width), BF16),
        grid_spec=grid_spec,
        compiler_params=_cparams("arbitrary", "arbitrary"),
        name="sb_sample",
    )(page_table, q_rhs, kv_new, *([cache] * n_pg))


def _compress_kernel(pt_ref, *refs, n_pg):
    pages = refs[:n_pg]
    w_refs = refs[n_pg:n_pg + 2]
    out_refs = refs[n_pg + 2:n_pg + 4]
    cpp = PAGE // CMP_STRIDE
    half = n_pg * cpp
    rows_per_tok = 4 * NSA_G
    for slot in range(2):
        acc = jnp.zeros((2 * half, 2 * CMP_HIDDEN), F32)
        for r in range(CMP_STRIDE):
            pieces = []
            for g in range(NSA_G):
                first = r * rows_per_tok + slot * NSA_G + g
                for p in range(n_pg):
                    pieces.append(pages[p].at[0][pl.ds(first, cpp, stride=CMP_STRIDE * rows_per_tok), :])
            x = jnp.concatenate(pieces, axis=0).astype(BF16)
            acc = acc + _dot(x, w_refs[slot][r])
        for g in range(NSA_G):
            out_refs[slot][0, g] = acc[g * half:(g + 1) * half]


def _compress(rows, table, w_k, w_v, n_pg):
    nb, n_pages = table.shape
    assert n_pages % n_pg == 0
    cpp = PAGE // CMP_STRIDE
    page_rows = rows.shape[1]

    def page_spec(p):
        return pl.BlockSpec((1, page_rows, HEAD_DIM), lambda b, j, pt: (pt[b, j * n_pg + p], 0, 0))

    w_spec = pl.BlockSpec((CMP_STRIDE, HEAD_DIM, 2 * CMP_HIDDEN), lambda b, j, pt: (0, 0, 0))
    out_spec = pl.BlockSpec((1, NSA_G, n_pg * cpp, 2 * CMP_HIDDEN), lambda b, j, pt: (b, 0, j, 0))
    out_sds = jax.ShapeDtypeStruct((nb, NSA_G, n_pages * cpp, 2 * CMP_HIDDEN), F32)
    grid_spec = pltpu.PrefetchScalarGridSpec(
        num_scalar_prefetch=1,
        grid=(nb, n_pages // n_pg),
        in_specs=[page_spec(p) for p in range(n_pg)] + [w_spec, w_spec],
        out_specs=[out_spec, out_spec],
    )
    return pl.pallas_call(
        functools.partial(_compress_kernel, n_pg=n_pg),
        out_shape=[out_sds, out_sds],
        grid_spec=grid_spec,
        compiler_params=_cparams("arbitrary", "arbitrary"),
        name="nsa_compress",
    )(table, *([rows] * n_pg), w_k, w_v)


def _gelu_tanh(x):
    return 0.5 * x * (1.0 + jnp.tanh(0.7978845608028654 * (x + 0.044715 * x * x * x)))


def _cmp_finish_kernel(*refs, nc_pad, n_tail):
    if n_tail:
        hk_ref, hv_ref, tk_ref, tv_ref = refs[:4]
        refs = refs[4:]
    else:
        hk_ref, hv_ref = refs[:2]
        tk_ref = tv_ref = None
        refs = refs[2:]
    posk_ref, posv_ref, w1k_ref, w1v_ref, w2k_ref, w2v_ref, g_ref, kc_ref, vc_ref = refs
    row = lax.broadcasted_iota(I32, (nc_pad, CMP_HIDDEN), 0)

    def chunks(h_ref, t_ref):
        parts = [h_ref[0, 0]]
        if t_ref is not None:
            tail = t_ref[0, 0]
            keep = lax.broadcasted_iota(I32, tail.shape, 0) < n_tail
            parts.append(jnp.where(keep, tail, 0.0))
        n = sum(p.shape[0] for p in parts)
        if nc_pad > n:
            parts.append(jnp.zeros((nc_pad - n, 2 * CMP_HIDDEN), F32))
        return jnp.concatenate(parts, axis=0) if len(parts) > 1 else parts[0]

    def branch(h_ref, t_ref, pos_ref, w1_ref, w2_ref):
        h = chunks(h_ref, t_ref)
        nxt = pltpu.roll(h[:, CMP_HIDDEN:], nc_pad - 1, 0)
        nxt = jnp.where(row == nc_pad - 1, 0.0, nxt)
        bias = _dot3(pos_ref[...], w1_ref[...])[0:1]
        hid = _gelu_tanh(h[:, :CMP_HIDDEN] + nxt + bias)
        return _dot(hid.astype(BF16), w2_ref[...].astype(BF16))

    kc_ref[0, 0] = _rms(branch(hk_ref, tk_ref, posk_ref, w1k_ref, w2k_ref), g_ref[...])
    vc_ref[0, 0] = branch(hv_ref, tv_ref, posv_ref, w1v_ref, w2v_ref)


def _cmp_finish(hk, hv, tails, n_tail, nc_pad, pos_k, pos_v, w1k, w1v, w2k, w2v, kc_g):
    nb, _, n_chunks, _ = hk.shape
    assert n_chunks % SUBLANES == 0 and nc_pad % SUBLANES == 0
    h_spec = pl.BlockSpec((1, 1, n_chunks, 2 * CMP_HIDDEN), lambda b, g: (b, g, 0, 0))
    t_specs = [pl.BlockSpec((1, 1) + t.shape[2:], lambda b, g: (b, g, 0, 0)) for t in tails]
    o_spec = pl.BlockSpec((1, 1, nc_pad, HEAD_DIM), lambda b, g: (b, g, 0, 0))
    o_sds = jax.ShapeDtypeStruct((nb, NSA_G, nc_pad, HEAD_DIM), F32)

    def full(a):
        return pl.BlockSpec(a.shape, lambda b, g: (0,) * a.ndim)

    pos_k8 = jnp.tile(pos_k.reshape(1, -1), (8, 1))
    pos_v8 = jnp.tile(pos_v.reshape(1, -1), (8, 1))
    kc_g = kc_g.reshape(1, HEAD_DIM)
    args = (pos_k8, pos_v8, w1k, w1v, w2k, w2v, kc_g)
    return pl.pallas_call(
        functools.partial(_cmp_finish_kernel, nc_pad=nc_pad, n_tail=n_tail if tails else 0),
        out_shape=[o_sds, o_sds],
        grid=(nb, NSA_G),
        in_specs=[h_spec, h_spec] + t_specs + [full(a) for a in args],
        out_specs=[o_spec, o_spec],
        compiler_params=_cparams("arbitrary", "arbitrary"),
        name="nsa_cmp_finish",
    )(hk, hv, *tails, *args)


def _cmp_select_kernel(slopes_ref, q_ref, kc_ref, vc_ref, oc_ref, sel_ref, *, tr, nc_pad, ns_pad, ns, base):
    g = pl.program_id(1)
    i = pl.program_id(2)
    q = q_ref[...]
    kc = kc_ref[0, 0].astype(BF16)
    vc = vc_ref[0, 0].astype(BF16)
    pos = base + i * tr + lax.broadcasted_iota(I32, (tr, 1), 0)
    end = lax.broadcasted_iota(I32, (1, nc_pad), 1) * CMP_STRIDE + (CMP_BLOCK - 1)
    dist = (pos - end).astype(F32)
    valid = dist >= 0.0
    score = jnp.zeros((tr, nc_pad), F32)
    for r in range(NSA_REP):
        sl = slice(r * HEAD_DIM, (r + 1) * HEAD_DIM)
        s = _dot_t(q[:, sl], kc) - slopes_ref[g, r] * dist
        p = _masked_softmax(s, valid)
        oc_ref[:, sl] = _dot(p.astype(BF16), vc)
        score = score + p
    n_id = lax.broadcasted_iota(I32, (nc_pad, ns_pad), 0)
    j_id = lax.broadcasted_iota(I32, (nc_pad, ns_pad), 1)
    group = jnp.where((n_id // SEL_RATIO) == j_id, 1.0, 0.0).astype(BF16)
    sc = _dot2_exact_rhs(score, group)
    j = lax.broadcasted_iota(I32, (tr, ns_pad), 1)
    cur = pos // SEL_BLOCK
    forced = (j == 0) | (j == cur) | (j == cur - 1)
    sc = jnp.where(j > cur, NEG_INF, jnp.where(forced, FORCE_SCORE, sc))
    sc = jnp.where(j < ns, sc, LOWEST)
    rank = jnp.zeros((tr, ns_pad), F32)
    for c in range(ns):
        other = sc[:, c:c + 1]
        ahead = (other > sc) | ((other == sc) & (c < j))
        rank = rank + jnp.where(ahead, 1.0, 0.0)
    sel_ref[0, 0] = jnp.where((rank < float(min(SEL_TOPK, ns))) & (j < ns), 1.0, 0.0)


def _cmp_select(q_n, kc, vc, slopes, nb, t, tr, ns, base):
    nc_pad = kc.shape[2]
    ns_pad = -(-ns // LANES) * LANES
    nt = t // tr
    width = NSA_REP * HEAD_DIM
    kv_spec = pl.BlockSpec((1, 1, nc_pad, HEAD_DIM), lambda b, g, i: (b, g, 0, 0))
    return pl.pallas_call(
        functools.partial(_cmp_select_kernel, tr=tr, nc_pad=nc_pad, ns_pad=ns_pad, ns=ns, base=base),
        out_shape=[jax.ShapeDtypeStruct(q_n.shape, F32),
                   jax.ShapeDtypeStruct((nb, NSA_G, t, ns_pad), F32)],
        grid=(nb, NSA_G, nt),
        in_specs=[pl.BlockSpec(memory_space=pltpu.SMEM),
                  pl.BlockSpec((tr, width), lambda b, g, i: (b * nt + i, g)), kv_spec, kv_spec],
        out_specs=[pl.BlockSpec((tr, width), lambda b, g, i: (b * nt + i, g)),
                   pl.BlockSpec((1, 1, tr, ns_pad), lambda b, g, i: (b, g, i, 0))],
        compiler_params=_cparams("arbitrary", "arbitrary", "arbitrary"),
        name="nsa_cmp_select",
    )(slopes, q_n, kc, vc)


def _block_expand(n_blocks, key_pos):
    j = lax.broadcasted_iota(I32, (n_blocks, key_pos.shape[1]), 0)
    return jnp.where(j == key_pos // SEL_BLOCK, 1.0, 0.0).astype(BF16)


def _biased_attention(qh, k, v, slope, neg_dist):
    s = _dot_t(qh, k) + slope * neg_dist
    e = jnp.exp(s - jnp.max(s, axis=-1, keepdims=True))
    return _dot(e.astype(BF16), v) / jnp.sum(e, axis=-1, keepdims=True)


def _nsa_prompt_kernel(slopes_ref, q_ref, sk_ref, sv_ref, wk_ref, wv_ref, sel_ref, oc_ref, gate_ref, o_ref,
                       part_ref, *, tq, t, n_cls):
    g = pl.program_id(1)
    i = pl.program_id(2)
    t0 = i * tq
    span = WINDOW + tq
    qpos = t0 + lax.broadcasted_iota(I32, (tq, 1), 0)
    gates = gate_ref[0]
    heads = [slice(r * HEAD_DIM, (r + 1) * HEAD_DIM) for r in range(NSA_REP)]

    wstart = pl.multiple_of(jnp.clip(t0 - WINDOW, 0, t - span), tq)
    wk = wk_ref[pl.ds(wstart, span), :].astype(BF16)
    wv = wv_ref[pl.ds(wstart, span), :].astype(BF16)
    dist_w = (qpos - (wstart + lax.broadcasted_iota(I32, (1, span), 1))).astype(F32)
    nd_w = jnp.where((dist_w >= 0.0) & (dist_w < float(WINDOW)), -dist_w, NEG_INF)
    for r, sl in enumerate(heads):
        o_w = _biased_attention(q_ref[:, sl], wk, wv, slopes_ref[g, r], nd_w)
        part_ref[:, sl] = gates[:, 3 * r:3 * r + 1] * oc_ref[:, sl] + gates[:, 3 * r + 2:3 * r + 3] * o_w

    def selected(n_keys):
        sk = sk_ref[pl.ds(0, n_keys), :].astype(BF16)
        sv = sv_ref[pl.ds(0, n_keys), :].astype(BF16)
        kpos = lax.broadcasted_iota(I32, (1, n_keys), 1)
        dist = (qpos - kpos).astype(F32)
        sel = sel_ref[0, 0].astype(BF16)
        picked = _dot(sel, _block_expand(sel.shape[1], kpos))
        nd = jnp.where((dist >= 0.0) & (picked > 0.5), -dist, NEG_INF)
        for r, sl in enumerate(heads):
            o_s = _biased_attention(q_ref[:, sl], sk, sv, slopes_ref[g, r], nd)
            o_ref[:, sl] = (part_ref[:, sl] + gates[:, 3 * r + 1:3 * r + 2] * o_s).astype(o_ref.dtype)

    chunk = t // n_cls
    cls = ((i + 1) * tq - 1) // chunk
    for c in range(n_cls):
        pl.when(cls == c)(functools.partial(selected, (c + 1) * chunk))


def _nsa_prompt(q_n, nsa_kv, win_kv, sel, o_c, gates, slopes, nb, t):
    tq = 128
    assert t >= WINDOW + tq and t % tq == 0
    n_cls = 4 if t % (4 * tq) == 0 else 1
    nt = t // tq
    width = NSA_REP * HEAD_DIM
    ns_pad = sel.shape[-1]
    q_spec = pl.BlockSpec((tq, width), lambda b, g, i: (b * nt + i, g))

    def kv_spec(slot):
        return pl.BlockSpec((t, HEAD_DIM), lambda b, g, i: (b, slot * NSA_G + g))

    return pl.pallas_call(
        functools.partial(_nsa_prompt_kernel, tq=tq, t=t, n_cls=n_cls),
        out_shape=jax.ShapeDtypeStruct(q_n.shape, BF16),
        scratch_shapes=[pltpu.VMEM((tq, width), F32)],
        grid=(nb, NSA_G, nt),
        in_specs=[pl.BlockSpec(memory_space=pltpu.SMEM), q_spec,
                  kv_spec(2), kv_spec(3), kv_spec(0), kv_spec(1),
                  pl.BlockSpec((1, 1, tq, ns_pad), lambda b, g, i: (b, g, i, 0)),
                  q_spec,
                  pl.BlockSpec((1, tq, LANES), lambda b, g, i: (g, b * nt + i, 0))],
        out_specs=q_spec,
        compiler_params=_cparams("arbitrary", "arbitrary", "arbitrary"),
        name="nsa_prompt",
    )(slopes, q_n, nsa_kv, nsa_kv, win_kv, win_kv, sel, o_c, gates)


def _nsa_sample_kernel(pt_ref, slopes_ref, q_ref, sel_ref, oc_ref, gate_ref, new_ref, wnew_ref, wcache_ref,
                       *refs, n_pg, dec, past):
    pages = refs[:n_pg]
    o_ref = refs[n_pg]
    m_ref, l_ref, acc_ref = refs[n_pg + 1:]
    j = pl.program_id(1)
    rows = NSA_REP * dec
    ns_pad = sel_ref.shape[-1]
    row = lax.broadcasted_iota(I32, (rows, 1), 0)
    qpos = past + row % dec

    @pl.when(j == 0)
    def _():
        m_ref[...] = jnp.full_like(m_ref, NEG_INF)
        l_ref[...] = jnp.zeros_like(l_ref)
        acc_ref[...] = jnp.zeros_like(acc_ref)

    def stack_heads(x, g):
        return jnp.concatenate(
            [x[:, (g * NSA_REP + r) * HEAD_DIM:(g * NSA_REP + r + 1) * HEAD_DIM] for r in range(NSA_REP)], axis=0)

    def slope_col(g):
        col = jnp.zeros((rows, 1), F32)
        for r in range(NSA_REP):
            col = jnp.where(row // dec == r, slopes_ref[g, r], col)
        return col

    def online(g, s, mask, v):
        s = jnp.where(mask, s, NEG_INF)
        m_old = m_ref[g]
        m_new = jnp.maximum(m_old, jnp.max(s, axis=-1, keepdims=True))
        alpha = jnp.exp(m_old - m_new)
        e = jnp.where(mask, jnp.exp(s - m_new), 0.0)
        l_ref[g] = alpha * l_ref[g] + jnp.sum(e, axis=-1, keepdims=True)
        acc_ref[g] = alpha * acc_ref[g] + _dot(e.astype(BF16), v)
        m_ref[g] = m_new

    def sel_rows(g):
        return jnp.concatenate([sel_ref[0, g]] * NSA_REP, axis=0).astype(BF16)

    def attend_selected(g, qg, k, v, kpos):
        dist = (qpos - kpos).astype(F32)
        picked = _dot(sel_rows(g), _block_expand(ns_pad, kpos))
        s = _dot_t(qg, k) - slope_col(g) * dist
        online(g, s, (dist >= 0.0) & (picked > 0.5), v)

    q = q_ref[...]
    n_keys = n_pg * PAGE
    kpos_pages = j * n_keys + lax.broadcasted_iota(I32, (1, n_keys), 1)
    rows_per_tok = 4 * NSA_G

    def page_rows(c):
        return jnp.concatenate([pages[p].at[0][pl.ds(c, PAGE, stride=rows_per_tok), :] for p in range(n_pg)],
                               axis=0).astype(BF16)

    for g in range(NSA_G):
        attend_selected(g, stack_heads(q, g), page_rows(2 * NSA_G + g), page_rows(3 * NSA_G + g), kpos_pages)

    @pl.when(j == pl.num_programs(1) - 1)
    def _():
        pad = jnp.zeros((PAGE - dec, HEAD_DIM), F32)
        kpos_new = past + lax.broadcasted_iota(I32, (1, PAGE), 1)
        win_rows = 2 * NSA_G
        nbuf = wcache_ref.shape[1] // win_rows
        kpos_w = jnp.concatenate([past - nbuf + lax.broadcasted_iota(I32, (1, nbuf), 1), kpos_new], axis=1)
        dist_w = (qpos - kpos_w).astype(F32)
        key_ok = jnp.concatenate([jnp.ones((1, nbuf), F32),
                                  jnp.where(lax.broadcasted_iota(I32, (1, PAGE), 1) < dec, 1.0, 0.0)], axis=1)
        mask_w = (dist_w >= 0.0) & (dist_w < float(WINDOW)) & (kpos_w >= 0) & (key_ok > 0.5)
        for g in range(NSA_G):
            qg = stack_heads(q, g)
            lane_k = (2 * NSA_G + g) * HEAD_DIM
            lane_v = (3 * NSA_G + g) * HEAD_DIM
            k_new = jnp.concatenate([new_ref[:, lane_k:lane_k + HEAD_DIM], pad], axis=0).astype(BF16)
            v_new = jnp.concatenate([new_ref[:, lane_v:lane_v + HEAD_DIM], pad], axis=0).astype(BF16)
            attend_selected(g, qg, k_new, v_new, kpos_new)
            o_s = acc_ref[g] / l_ref[g]
            lw_k = g * HEAD_DIM
            lw_v = (NSA_G + g) * HEAD_DIM
            wk = jnp.concatenate([wcache_ref.at[0][pl.ds(g, nbuf, stride=win_rows), :],
                                  wnew_ref[:, lw_k:lw_k + HEAD_DIM], pad], axis=0).astype(BF16)
            wv = jnp.concatenate([wcache_ref.at[0][pl.ds(NSA_G + g, nbuf, stride=win_rows), :],
                                  wnew_ref[:, lw_v:lw_v + HEAD_DIM], pad], axis=0).astype(BF16)
            p_w = _masked_softmax(_dot_t(qg, wk) - slope_col(g) * dist_w, mask_w)
            o_w = _dot(p_w.astype(BF16), wv)
            gates = gate_ref[g]
            for r in range(NSA_REP):
                rs = slice(r * dec, (r + 1) * dec)
                sl = slice((g * NSA_REP + r) * HEAD_DIM, (g * NSA_REP + r + 1) * HEAD_DIM)
                out = (gates[:, 3 * r:3 * r + 1] * oc_ref[:, sl] + gates[:, 3 * r + 1:3 * r + 2] * o_s[rs]
                       + gates[:, 3 * r + 2:3 * r + 3] * o_w[rs])
                o_ref[:, sl] = out.astype(o_ref.dtype)


def _nsa_sample(q_n, sel, o_c, gates, nsa_new, win_new, win_cache, cache, page_table, slopes, dec):
    nb, n_pages = page_table.shape
    n_pg = _pick_tile(n_pages, 8)
    past = n_pages * PAGE
    rows = NSA_REP * dec
    ns_pad = sel.shape[-1]
    lanes = nsa_new.shape[-1]
    wl = win_new.shape[-1]
    page_rows = cache.shape[1]

    def page_spec(p):
        return pl.BlockSpec((1, page_rows, HEAD_DIM), lambda b, j, pt: (pt[b, j * n_pg + p], 0, 0))

    tok = lambda w: pl.BlockSpec((dec, w), lambda b, j, pt: (b, 0))
    grid_spec = pltpu.PrefetchScalarGridSpec(
        num_scalar_prefetch=1,
        grid=(nb, n_pages // n_pg),
        in_specs=[pl.BlockSpec(memory_space=pltpu.SMEM),
                  tok(NSA_HEADS * HEAD_DIM),
                  pl.BlockSpec((1, NSA_G, dec, ns_pad), lambda b, j, pt: (b, 0, 0, 0)),
                  tok(NSA_HEADS * HEAD_DIM),
                  pl.BlockSpec((NSA_G, dec, LANES), lambda b, j, pt: (0, b, 0)),
                  tok(lanes), tok(wl),
                  pl.BlockSpec((1, win_cache.shape[1], HEAD_DIM), lambda b, j, pt: (b, 0, 0))]
        + [page_spec(p) for p in range(n_pg)],
        out_specs=tok(NSA_HEADS * HEAD_DIM),
        scratch_shapes=[pltpu.VMEM((NSA_G, rows, 1), F32), pltpu.VMEM((NSA_G, rows, 1), F32),
                        pltpu.VMEM((NSA_G, rows, HEAD_DIM), F32)],
    )
    return pl.pallas_call(
        functools.partial(_nsa_sample_kernel, n_pg=n_pg, dec=dec, past=past),
        out_shape=jax.ShapeDtypeStruct(q_n.shape, BF16),
        grid_spec=grid_spec,
        compiler_params=_cparams("arbitrary", "arbitrary"),
        name="nsa_sample",
    )(page_table, slopes, q_n, sel, o_c, gates, nsa_new, win_new, win_cache, *([cache] * n_pg))


def _resid_route_kernel(x_ref, mix_ref, gate_ref, g_ref, scale_ref, shift_ref, wr_hi_ref, wr_lo_ref, br_ref,
                        x1_ref, h_ref, rid_ref, rw_ref):
    x1 = x_ref[...] + _as2d(gate_ref[...]) * mix_ref[...]
    x1_ref[...] = x1
    h = _rms(x1, g_ref[...]) * (1.0 + _as2d(scale_ref[...])) + _as2d(shift_ref[...])
    h_ref[...] = h
    h_hi, h_lo = _split(h)
    logits = _dot(h_hi, wr_hi_ref[...]) + _dot(h_lo, wr_hi_ref[...]) + _dot(h_hi, wr_lo_ref[...]) + br_ref[...]
    tm = logits.shape[0]
    lane = lax.broadcasted_iota(I32, (tm, LANES), 1).astype(F32)
    far = float(LANES)

    def first_lane(cond):
        return jnp.min(jnp.where(cond, lane, far), axis=-1, keepdims=True)

    in_groups = lane < float(MOE_GROUPS)
    lg = jnp.where(in_groups, logits, LOWEST)
    g_max = jnp.max(lg, axis=-1, keepdims=True)
    g_sum = jnp.sum(jnp.where(in_groups, jnp.exp(lg - g_max), 0.0), axis=-1, keepdims=True)
    g_top = 1.0 / g_sum
    g_idx = first_lane(in_groups & (lg == g_max))
    lo = float(MOE_GROUPS) + float(EPG) * g_idx
    in_experts = (lane >= lo) & (lane < lo + float(EPG))
    le = jnp.where(in_experts, logits, LOWEST)
    e_max = jnp.max(le, axis=-1, keepdims=True)
    e_sum = jnp.sum(jnp.where(in_experts, jnp.exp(le - e_max), 0.0), axis=-1, keepdims=True)
    i1 = first_lane(in_experts & (le == e_max))
    le2 = jnp.where(lane == i1, LOWEST, le)
    e_max2 = jnp.max(le2, axis=-1, keepdims=True)
    i2 = first_lane(in_experts & (lane != i1) & (le2 == e_max2))
    p1 = 1.0 / e_sum
    p2 = jnp.exp(e_max2 - e_max) / e_sum
    w1 = p1 / (p1 + p2) * g_top
    w2 = p2 / (p1 + p2) * g_top
    first = lane == 0.0
    second = lane == 1.0
    ids = jnp.where(first, i1 - float(MOE_GROUPS), jnp.where(second, i2 - float(MOE_GROUPS), 0.0))
    rid_ref[...] = ids.astype(I32)
    rw_ref[...] = jnp.where(first, w1, jnp.where(second, w2, 0.0))


def _resid_route(x, mix, gain, mod, per_token, nb, wr_hi, wr_lo, br, tm):
    m, d = x.shape
    nt = m // nb // tm
    row = pl.BlockSpec((tm, d), lambda b, i: (b * nt + i, 0))
    lane_row = pl.BlockSpec((tm, LANES), lambda b, i: (b * nt + i, 0))
    const = lambda a: pl.BlockSpec(a.shape, lambda b, i: (0, 0))
    gain = gain.reshape(1, d)
    return pl.pallas_call(
        _resid_route_kernel,
        out_shape=[jax.ShapeDtypeStruct((m, d), F32), jax.ShapeDtypeStruct((m, d), F32),
                   jax.ShapeDtypeStruct((m, LANES), I32), jax.ShapeDtypeStruct((m, LANES), F32)],
        grid=(nb, nt),
        in_specs=[row, row, _mod_spec(per_token, tm, d, 2, nt), const(gain),
                  _mod_spec(per_token, tm, d, 4, nt), _mod_spec(per_token, tm, d, 3, nt),
                  const(wr_hi), const(wr_lo), const(br)],
        out_specs=[row, row, lane_row, lane_row],
        compiler_params=_cparams("arbitrary", "arbitrary"),
        name="resid_route",
    )(x, mix, mod, gain, mod, mod, wr_hi, wr_lo, br)


def _moe_kernel(texp_ref, nused_ref, src_ref, dst_ref, h_hbm, roww_ref, wg_ref, wu_ref, wd_ref, y_hbm,
                xbuf, xbf, acc, sem_in, sem_out, *, tm):
    i = pl.program_id(0)
    f = pl.program_id(1)
    n_used = nused_ref[0]
    last_f = pl.num_programs(1) - 1
    slot = i % 2

    def gather_copy(r, src_row, s):
        return pltpu.make_async_copy(h_hbm.at[pl.ds(src_row, 1)], xbuf.at[s, pl.ds(r, 1)], sem_in.at[s])

    def scatter_copy(r, dst_row):
        return pltpu.make_async_copy(acc.at[pl.ds(r, 1)], y_hbm.at[pl.ds(dst_row, 1)], sem_out)

    def each_row(fn):
        def body(r, c):
            fn(r)
            return c

        lax.fori_loop(0, tm, body, 0, unroll=8)

    def gather_start(tile, s):
        each_row(lambda r: gather_copy(r, src_ref[tile * tm + r], s).start())

    def gather_wait(s):
        each_row(lambda r: gather_copy(r, 0, s).wait())

    def scatter(tile, start):
        def one(r):
            dst = dst_ref[tile * tm + r]

            @pl.when(dst >= 0)
            def _():
                if start:
                    scatter_copy(r, dst).start()
                else:
                    scatter_copy(r, 0).wait()

        each_row(one)

    @pl.when(i < n_used)
    def _():
        @pl.when(f == 0)
        def _():
            @pl.when(i == 0)
            def _():
                gather_start(0, 0)

            gather_wait(slot)
            xbf[...] = xbuf[slot].astype(BF16)

        @pl.when((f == last_f) & (i + 1 < n_used))
        def _():
            gather_start(i + 1, 1 - slot)

        x = xbf[...]
        gt = _dot(x, wg_ref[0].astype(BF16))
        up = _dot(x, wu_ref[0].astype(BF16))
        hid = gt * _sigmoid(gt) * up
        part = _dot(hid.astype(BF16), wd_ref[0].astype(BF16))

        @pl.when(f == 0)
        def _():
            @pl.when(i > 0)
            def _():
                scatter(i - 1, start=False)

            acc[...] = part

        @pl.when(f == last_f)
        def _():
            acc[...] = (acc[...] + part) * roww_ref[...]
            scatter(i, start=True)

            @pl.when(i == n_used - 1)
            def _():
                scatter(i, start=False)


def _moe(h_all, route_id, route_w, w_gate, w_up, w_down):
    n, d = h_all.shape
    tm = 256
    n_ff = 2
    ff = EXPERT_FF // n_ff
    a = 2 * n
    t_max = (a + N_EXPERTS * (tm - 1)) // tm
    e_flat = route_id.reshape(-1)
    onehot = (e_flat[:, None] == jnp.arange(N_EXPERTS, dtype=I32)[None, :]).astype(I32)
    running = jnp.cumsum(onehot, axis=0)
    counts = running[-1]
    tiles_per = (counts + tm - 1) // tm
    tiles_end = jnp.cumsum(tiles_per)
    n_used = tiles_end[-1]
    seg_start = (tiles_end - tiles_per) * tm
    dest = jnp.sum(onehot * (running - 1 + seg_start[None, :]), axis=1)
    owner = jnp.zeros((t_max * tm,), I32).at[dest].set(jnp.arange(1, a + 1, dtype=I32)) - 1
    valid = owner >= 0
    src_row = jnp.where(valid, owner // 2, 0)
    dst_row = jnp.where(valid, (owner % 2) * n + owner // 2, -1)
    row_w = jnp.where(valid, route_w.reshape(-1)[jnp.maximum(owner, 0)], 0.0).reshape(-1, 1)
    tile = jnp.minimum(jnp.arange(t_max, dtype=I32), n_used - 1)
    tile_e = jnp.minimum(jnp.sum((tiles_end[None, :] <= tile[:, None]).astype(I32), axis=1), N_EXPERTS - 1)

    def half(i, f, nu):
        return jnp.where(i < nu[0], f, n_ff - 1)

    assert n_ff >= 2
    grid_spec = pltpu.PrefetchScalarGridSpec(
        num_scalar_prefetch=4,
        grid=(t_max, n_ff),
        in_specs=[pl.BlockSpec(memory_space=pl.ANY),
                  pl.BlockSpec((tm, 1), lambda i, f, te, nu, sr, ds: (i, 0)),
                  pl.BlockSpec((1, d, ff), lambda i, f, te, nu, sr, ds: (te[i], 0, half(i, f, nu))),
                  pl.BlockSpec((1, d, ff), lambda i, f, te, nu, sr, ds: (te[i], 0, half(i, f, nu))),
                  pl.BlockSpec((1, ff, d), lambda i, f, te, nu, sr, ds: (te[i], half(i, f, nu), 0))],
        out_specs=pl.BlockSpec(memory_space=pl.ANY),
        scratch_shapes=[pltpu.VMEM((2, tm, d), F32), pltpu.VMEM((tm, d), BF16), pltpu.VMEM((tm, d), F32),
                        pltpu.SemaphoreType.DMA((2,)), pltpu.SemaphoreType.DMA(())],
    )
    return pl.pallas_call(
        functools.partial(_moe_kernel, tm=tm),
        out_shape=jax.ShapeDtypeStruct((a, d), F32),
        grid_spec=grid_spec,
        compiler_params=pltpu.CompilerParams(dimension_semantics=("arbitrary", "arbitrary"),
                                             vmem_limit_bytes=VMEM_LIMIT, has_side_effects=True),
        name="moe",
    )(tile_e, n_used.reshape(1).astype(I32), src_row, dst_row, h_all, row_w, w_gate, w_up, w_down)


def _final_kernel(x_ref, gate_ref, ya_ref, yb_ref, o_ref):
    o_ref[...] = x_ref[...] + _as2d(gate_ref[...]) * (ya_ref[...] + yb_ref[...])


def _final(x1, mod, per_token, nb, y, row_off, n_all, tm):
    m, d = x1.shape
    nt = m // nb // tm
    off_a = row_off // tm
    off_b = (n_all + row_off) // tm
    row = pl.BlockSpec((tm, d), lambda b, i: (b * nt + i, 0))
    return pl.pallas_call(
        _final_kernel,
        out_shape=jax.ShapeDtypeStruct((m, d), F32),
        grid=(nb, nt),
        in_specs=[row, _mod_spec(per_token, tm, d, 5, nt),
                  pl.BlockSpec((tm, d), lambda b, i: (off_a + b * nt + i, 0)),
                  pl.BlockSpec((tm, d), lambda b, i: (off_b + b * nt + i, 0))],
        out_specs=row,
        compiler_params=_cparams("arbitrary", "arbitrary"),
        name="final_residual",
    )(x1, mod, y, y)


def _pick_tile(m, pref):
    t = pref
    while m % t:
        t //= 2
    return t


def _project(h, w_in, gains, d):
    m = h.shape[0]
    tm = _pick_tile(m, 1024)
    tn = 512
    sbw = SB_HEADS * HEAD_DIM
    nw = NSA_HEADS * HEAD_DIM
    kvw = NSA_G * HEAD_DIM
    o_q, o_kv, o_qn, o_nkv, o_win, o_gn = 0, sbw, 3 * sbw, 3 * sbw + nw, 3 * sbw + nw + 4 * kvw, 3 * sbw + nw + 6 * kvw
    mm = functools.partial(_matmul, [h], [w_in], tm=tm, tn=tn)
    q_sb = mm(o_q, sbw, [], _ep_scaled, BF16, name="proj_q_sb")
    kv_sb = mm(o_kv, 2 * sbw, [], _ep_plain, F32, name="proj_kv_sb")

    def normed(off, width, gain_cols, flag_cols, post, dtype, name):
        return mm(off, width, [("col", gain_cols.reshape(1, width)), ("col", flag_cols.reshape(1, width))],
                  functools.partial(_ep_groupnorm, post_scale=post), dtype, name=name)

    ones = lambda k: jnp.ones((k * HEAD_DIM,), F32)
    zeros = lambda k: jnp.zeros((k * HEAD_DIM,), F32)
    q_n = normed(o_qn, nw, jnp.tile(gains["q"], NSA_HEADS), ones(NSA_HEADS), SCALE, BF16, "proj_q_nsa")
    nsa_kv = normed(o_nkv, 4 * kvw,
                    jnp.concatenate([ones(2 * NSA_G), jnp.tile(gains["ks"], NSA_G), ones(NSA_G)]),
                    jnp.concatenate([zeros(2 * NSA_G), ones(NSA_G), zeros(NSA_G)]), 1.0, F32, "proj_nsa_kv")
    win_kv = normed(o_win, 2 * kvw, jnp.concatenate([jnp.tile(gains["kw"], NSA_G), ones(NSA_G)]),
                    jnp.concatenate([ones(NSA_G), zeros(NSA_G)]), 1.0, F32, "proj_win_kv")
    g_n = mm(o_gn, tn, [], _ep_sigmoid, F32, name="proj_g_nsa")[:, :3 * NSA_HEADS]
    w_gates = w_in[:, o_gn + 3 * NSA_HEADS:]
    gate_sb = _matmul([h], [w_gates], 0, d, [], _ep_sigmoid, BF16, tm, tn, "proj_gate_sb")
    gate_n = _matmul([h], [w_gates], d, d, [], _ep_sigmoid, BF16, tm, tn, "proj_gate_nsa")
    gates = jnp.pad(g_n.reshape(m, NSA_G, 3 * NSA_REP).transpose(1, 0, 2),
                    ((0, 0), (0, 0), (0, LANES - 3 * NSA_REP)))
    return q_sb, kv_sb, q_n, nsa_kv, win_kv, gates, gate_sb, gate_n


def _compress_weights(w1):
    half = CMP_STRIDE * HEAD_DIM
    a = w1[:half].reshape(CMP_STRIDE, HEAD_DIM, CMP_HIDDEN)
    b = w1[half:].reshape(CMP_STRIDE, HEAD_DIM, CMP_HIDDEN)
    return jnp.concatenate([a, b], axis=-1).astype(BF16)


def _merge_out(o_sb, o_n, gate_sb, gate_n, w):
    m = o_sb.shape[0]
    d = w["w_out"].shape[1]
    tm = _pick_tile(m, 1024)
    mixed = _matmul([o_sb, o_n], [w["w_br_sb"], w["w_br_nsa"]], 0, d, [("tile", gate_sb), ("tile", gate_n)],
                    _ep_gated_pair, BF16, tm, 512, "merge")
    return _matmul([mixed], [w["w_out"]], 0, d, [], _ep_plain, F32, tm, 512, "out_proj")


def _alibi_slopes():
    h = jnp.arange(1, NSA_HEADS + 1, dtype=F32)
    return jnp.exp2(-8.0 * h / NSA_HEADS).reshape(NSA_G, NSA_REP)


def _layer(xp, xs, c_all, sb_cache, nsa_cache, win_cache, page_table, w, nb, t, nbs, dec):
    d = xp.shape[1]
    n_p, n_s = xp.shape[0], xs.shape[0]
    slopes = _alibi_slopes()
    mod = _adaln(c_all, w["w_ada"], w["b_ada"])
    mod_p = mod[:nb].reshape(nb, 1, 6 * d)
    mod_s = jnp.repeat(mod[nb:nb + nbs], dec, axis=0)
    gains = {"q": w["nsa_q_g"], "ks": w["nsa_ks_g"], "kw": w["nsa_kw_g"]}
    w1k, w1v = _compress_weights(w["cmp_w1_k"]), _compress_weights(w["cmp_w1_v"])
    cpp = PAGE // CMP_STRIDE

    tm_p = _pick_tile(t, 256)
    h_p = _modulate(xp, w["norm1_g"], mod_p, False, nb, 1, 0, tm_p)
    q_sb, kv_sb_p, q_n, nsa_kv_p, win_kv_p, gates, gate_sb, gate_n = _project(h_p, w["w_in"], gains, d)
    o_sb = _sb_prompt(q_sb, kv_sb_p, nb, t)
    ident = jnp.arange(nb * (t // PAGE), dtype=I32).reshape(nb, t // PAGE)
    cmp_w = (w["cmp_pos_k"], w["cmp_pos_v"], w["cmp_w1_k"], w["cmp_w1_v"], w["cmp_w2_k"], w["cmp_w2_v"],
             w["nsa_kc_g"])
    nsa_rows_p = nsa_kv_p.reshape(nb, t, 4, NSA_G, HEAD_DIM)
    hk, hv = _compress(nsa_rows_p.reshape(nb * t // PAGE, PAGE * 4 * NSA_G, HEAD_DIM), ident, w1k, w1v,
                       _pick_tile(t // PAGE, 8))
    nc_p = t // CMP_STRIDE
    kc, vc = _cmp_finish(hk, hv, (), 0, -(-nc_p // LANES) * LANES, *cmp_w)
    o_c, sel = _cmp_select(q_n, kc, vc, slopes, nb, t, 128, t // SEL_BLOCK, 0)
    o_n = _nsa_prompt(q_n, nsa_kv_p, win_kv_p, sel, o_c, gates, slopes, nb, t)
    mix_p = _merge_out(o_sb, o_n, gate_sb, gate_n, w)

    n_pages = page_table.shape[1]
    past = n_pages * PAGE
    h_s = _modulate(xs, w["norm1_g"], mod_s, True, 1, 1, 0, n_s)
    q_sb, kv_sb_s, q_n, nsa_kv_s, win_kv_s, gates, gate_sb, gate_n = _project(h_s, w["w_in"], gains, d)
    eye = jnp.eye(SB_HEADS, dtype=BF16)
    q_rhs = (q_sb.reshape(nbs, dec, SB_HEADS, HEAD_DIM).transpose(0, 2, 3, 1)[:, :, :, None, :]
             * eye[None, :, None, :, None]).reshape(nbs, SB_HEADS, HEAD_DIM, SB_HEADS * dec)
    o_sb = _sb_sample(q_rhs, kv_sb_s, sb_cache.reshape(sb_cache.shape[0], -1, HEAD_DIM), page_table, dec)
    nsa_pages = nsa_cache.reshape(nsa_cache.shape[0], -1, HEAD_DIM)
    hk, hv = _compress(nsa_pages, page_table, w1k, w1v, _pick_tile(n_pages, 8))
    nsa_rows_s = nsa_kv_s.reshape(nbs, dec, 4, NSA_G, HEAD_DIM)
    tail = jnp.pad(nsa_rows_s.reshape(nbs, dec * 4 * NSA_G, HEAD_DIM), ((0, 0), (0, (PAGE - dec) * 4 * NSA_G), (0, 0)))
    n_tot = past + dec + (-(past + dec)) % SEL_BLOCK
    nc = n_tot // CMP_STRIDE
    nc_pad = -(-nc // LANES) * LANES
    n_tail = nc - n_pages * cpp
    assert 0 <= n_tail <= cpp
    tails = _compress(tail, jnp.arange(nbs, dtype=I32).reshape(nbs, 1), w1k, w1v, 1) if n_tail else ()
    kc, vc = _cmp_finish(hk, hv, tuple(tails), n_tail, nc_pad, *cmp_w)
    o_c, sel = _cmp_select(q_n, kc, vc, slopes, nbs, dec, dec, n_tot // SEL_BLOCK, past)
    o_n = _nsa_sample(q_n, sel, o_c, gates, nsa_kv_s, win_kv_s, win_cache.reshape(nbs, -1, HEAD_DIM),
                      nsa_pages, page_table, slopes, dec)
    mix_s = _merge_out(o_sb, o_n, gate_sb, gate_n, w)

    wr = jnp.pad(jnp.concatenate([w["w_route_group"], w["w_route_expert"]], axis=1),
                 ((0, 0), (0, LANES - MOE_GROUPS - N_EXPERTS)))
    br = jnp.pad(jnp.concatenate([w["b_route_group"], w["b_route_expert"]]),
                 (0, LANES - MOE_GROUPS - N_EXPERTS)).reshape(1, LANES)
    wr_hi, wr_lo = _split(wr)
    tm_r = _pick_tile(t, 128)
    x1_p, h2_p, rid_p, rw_p = _resid_route(xp, mix_p, w["norm2_g"], mod_p, False, nb, wr_hi, wr_lo, br, tm_r)
    x1_s, h2_s, rid_s, rw_s = _resid_route(xs, mix_s, w["norm2_g"], mod_s, True, 1, wr_hi, wr_lo, br,
                                           _pick_tile(n_s, 128))
    h2 = jnp.concatenate([h2_p, h2_s], axis=0)
    rid = jnp.concatenate([rid_p[:, :2], rid_s[:, :2]], axis=0)
    rw = jnp.concatenate([rw_p[:, :2], rw_s[:, :2]], axis=0)
    y = _moe(h2, rid, rw, w["w_exp_gate"], w["w_exp_up"], w["w_exp_down"])
    n_all = n_p + n_s
    tm_f = _pick_tile(n_s, _pick_tile(t, 256))
    y_p = _final(x1_p, mod_p, False, nb, y, 0, n_all, tm_f)
    y_s = _final(x1_s, mod_s, True, 1, y, n_p, n_all, tm_f)

    win_state_p = win_kv_p.reshape(nb, t, -1)[:, t - min(WINDOW, t):]
    win_state_s = jnp.concatenate([win_cache[:, dec:], win_kv_s.reshape((nbs, dec) + win_cache.shape[2:])], axis=1)
    state = (kv_sb_p, nsa_rows_p, win_state_p, kv_sb_s, nsa_rows_s, win_state_s)
    return y_p, y_s, state


def kernel(x_prompt, x_sample, c_prompt, c_sample, cache_sb_kv, cache_nsa_kv, cache_win_kv, page_table, norm1_g, norm2_g, w_ada, b_ada, w_in, nsa_q_g, nsa_kc_g, nsa_ks_g, nsa_kw_g, cmp_pos_k, cmp_pos_v, cmp_w1_k, cmp_w2_k, cmp_w1_v, cmp_w2_v, w_br_sb, w_br_nsa, w_out, w_route_group, b_route_group, w_route_expert, b_route_expert, w_exp_gate, w_exp_up, w_exp_down):
    nb, t, d = x_prompt.shape
    nbs, dec, _ = x_sample.shape
    depth = w_in.shape[0]
    weights = dict(norm1_g=norm1_g, norm2_g=norm2_g, w_ada=w_ada, b_ada=b_ada, w_in=w_in, nsa_q_g=nsa_q_g,
                   nsa_kc_g=nsa_kc_g, nsa_ks_g=nsa_ks_g, nsa_kw_g=nsa_kw_g, cmp_pos_k=cmp_pos_k,
                   cmp_pos_v=cmp_pos_v, cmp_w1_k=cmp_w1_k, cmp_w2_k=cmp_w2_k, cmp_w1_v=cmp_w1_v,
                   cmp_w2_v=cmp_w2_v, w_br_sb=w_br_sb, w_br_nsa=w_br_nsa, w_out=w_out,
                   w_route_group=w_route_group, b_route_group=b_route_group, w_route_expert=w_route_expert,
                   b_route_expert=b_route_expert, w_exp_gate=w_exp_gate, w_exp_up=w_exp_up, w_exp_down=w_exp_down)
    n_c = nb + nbs
    c_all = jnp.pad(jnp.concatenate([c_prompt, c_sample], axis=0), ((0, (-n_c) % 8), (0, 0)))
    xp = x_prompt.reshape(nb * t, d)
    xs = x_sample.reshape(nbs * dec, d)
    states = []
    for layer in range(depth):
        w = {k: v[layer] for k, v in weights.items()}
        xp, xs, st = _layer(xp, xs, c_all, cache_sb_kv[layer], cache_nsa_kv[layer], cache_win_kv[layer],
                            page_table, w, nb, t, nbs, dec)
        states.append(st)
    g = NSA_G
    stack = lambda k, shape: jnp.stack([s[k].reshape(shape) for s in states])
    return (xp.reshape(nb, t, d), xs.reshape(nbs, dec, d),
            stack(0, (nb, t, 2, SB_HEADS, HEAD_DIM)),
            stack(1, (nb, t, 4, g, HEAD_DIM)),
            stack(2, (nb, min(WINDOW, t), 2, g, HEAD_DIM)),
            stack(3, (nbs, dec, 2, SB_HEADS, HEAD_DIM)),
            stack(4, (nbs, dec, 4, g, HEAD_DIM)),
            stack(5, (nbs, cache_win_kv.shape[2], 2, g, HEAD_DIM)))
```

```python
import functools

import jax
import jax.numpy as jnp
from jax import lax
from jax.experimental import pallas as pl
from jax.experimental.pallas import tpu as pltpu

F32 = jnp.float32
BF16 = jnp.bfloat16
I32 = jnp.int32

HEAD_DIM = 128
SB_HEADS = 16
NSA_HEADS = 16
NSA_G = 2
NSA_REP = NSA_HEADS // NSA_G
CMP_STRIDE = 16
CMP_BLOCK = 2 * CMP_STRIDE
CMP_HIDDEN = 256
SEL_BLOCK = 64
SEL_RATIO = SEL_BLOCK // CMP_STRIDE
SEL_TOPK = 16
WINDOW = 512
PAGE = 128
MOE_GROUPS = 4
EPG = 8
N_EXPERTS = MOE_GROUPS * EPG
EXPERT_FF = 512
NORM_EPS = 1e-6
NEG_INF = -1e30
FORCE_SCORE = 1e4
LOWEST = -3e38
SCALE = HEAD_DIM ** -0.5
LANES = 128
SUBLANES = 8
VMEM_LIMIT = 48 * 1024 * 1024
MOE_VMEM_LIMIT = 56 * 1024 * 1024


def _cparams(*sem):
    return pltpu.CompilerParams(dimension_semantics=sem, vmem_limit_bytes=VMEM_LIMIT)


def _dot(a, b):
    return jnp.dot(a, b, preferred_element_type=F32)


def _dot_t(a, b):
    return lax.dot_general(a, b, (((1,), (1,)), ((), ())), preferred_element_type=F32)


def _split(x):
    hi = x.astype(BF16)
    lo = (x - hi.astype(F32)).astype(BF16)
    return hi, lo


def _dot3(a, b):
    a_hi, a_lo = _split(a)
    b_hi, b_lo = _split(b)
    return _dot(a_hi, b_hi) + _dot(a_lo, b_hi) + _dot(a_hi, b_lo)


def _dot2_exact_rhs(a, b_bf16):
    a_hi, a_lo = _split(a)
    return _dot(a_hi, b_bf16) + _dot(a_lo, b_bf16)


def _sigmoid(x):
    return 1.0 / (1.0 + jnp.exp(-x))


def _softplus(x):
    return jnp.maximum(x, 0.0) + jnp.log(1.0 + jnp.exp(-jnp.abs(x)))


def _rms(x, gain):
    return x * lax.rsqrt(jnp.mean(x * x, axis=-1, keepdims=True) + NORM_EPS) * gain


def _masked_softmax(s, mask):
    s = jnp.where(mask, s, NEG_INF)
    m = jnp.max(s, axis=-1, keepdims=True)
    e = jnp.where(mask, jnp.exp(s - m), 0.0)
    l = jnp.sum(e, axis=-1, keepdims=True)
    return e / jnp.where(l > 0.0, l, 1.0)


def _as2d(v):
    return v.reshape(v.shape[-2], v.shape[-1])


def _adaln_kernel(c_ref, w_ref, b_ref, o_ref):
    c = c_ref[...]
    o_ref[...] = _dot3(c * _sigmoid(c), w_ref[...]) + b_ref[...]


def _adaln(c_pad, w_ada, b_ada):
    m, d = c_pad.shape
    n = w_ada.shape[1]
    tn = 512
    return pl.pallas_call(
        _adaln_kernel,
        out_shape=jax.ShapeDtypeStruct((m, n), F32),
        grid=(n // tn,),
        in_specs=[pl.BlockSpec((m, d), lambda j: (0, 0)),
                  pl.BlockSpec((d, tn), lambda j: (0, j)),
                  pl.BlockSpec((1, tn), lambda j: (0, j))],
        out_specs=pl.BlockSpec((m, tn), lambda j: (0, j)),
        compiler_params=_cparams("arbitrary"),
        name="adaln",
    )(c_pad, w_ada, b_ada.reshape(1, n))


def _modulate_kernel(x_ref, g_ref, scale_ref, shift_ref, h_ref):
    h = _rms(x_ref[...], g_ref[...]) * (1.0 + _as2d(scale_ref[...])) + _as2d(shift_ref[...])
    h_ref[...] = h.astype(h_ref.dtype)


def _mod_spec(per_token, tm, d, col, nt):
    if per_token:
        return pl.BlockSpec((tm, d), lambda b, i: (b * nt + i, col))
    return pl.BlockSpec((1, 1, d), lambda b, i: (b, 0, col))


def _modulate(x, gain, mod, per_token, nb, scale_col, shift_col, tm):
    m, d = x.shape
    nt = m // nb // tm
    row = pl.BlockSpec((tm, d), lambda b, i: (b * nt + i, 0))
    return pl.pallas_call(
        _modulate_kernel,
        out_shape=jax.ShapeDtypeStruct((m, d), BF16),
        grid=(nb, nt),
        in_specs=[row, pl.BlockSpec((1, d), lambda b, i: (0, 0)),
                  _mod_spec(per_token, tm, d, scale_col, nt), _mod_spec(per_token, tm, d, shift_col, nt)],
        out_specs=row,
        compiler_params=_cparams("arbitrary", "arbitrary"),
        name="modulate",
    )(x, gain.reshape(1, d), mod, mod)


def _mm_kernel(*refs, n_pairs, n_extra, epilogue):
    xs = refs[:n_pairs]
    ws = refs[n_pairs:2 * n_pairs]
    extras = refs[2 * n_pairs:2 * n_pairs + n_extra]
    o_ref = refs[2 * n_pairs + n_extra]
    wbfs = refs[2 * n_pairs + n_extra + 1:]

    @pl.when(pl.program_id(1) == 0)
    def _():
        for w, wb in zip(ws, wbfs):
            wb[...] = w[...].astype(BF16)

    accs = [_dot(x[...], wb[...]) for x, wb in zip(xs, wbfs)]
    o_ref[...] = epilogue(accs, [e[...] for e in extras]).astype(o_ref.dtype)


def _matmul(xs, ws, col_off, n_cols, extras, epilogue, out_dtype, tm, tn, name):
    m = xs[0].shape[0]
    assert m % tm == 0 and n_cols % tn == 0 and col_off % tn == 0
    joff = col_off // tn
    in_specs = [pl.BlockSpec((tm, x.shape[1]), lambda j, i: (i, 0)) for x in xs]
    in_specs += [pl.BlockSpec((w.shape[0], tn), lambda j, i: (0, joff + j)) for w in ws]
    for kind, arr in extras:
        if kind == "col":
            in_specs.append(pl.BlockSpec((1, tn), lambda j, i: (0, j)))
        else:
            in_specs.append(pl.BlockSpec((tm, tn), lambda j, i: (i, j)))
    return pl.pallas_call(
        functools.partial(_mm_kernel, n_pairs=len(xs), n_extra=len(extras), epilogue=epilogue),
        out_shape=jax.ShapeDtypeStruct((m, n_cols), out_dtype),
        grid=(n_cols // tn, m // tm),
        in_specs=in_specs,
        out_specs=pl.BlockSpec((tm, tn), lambda j, i: (i, j)),
        scratch_shapes=[pltpu.VMEM((w.shape[0], tn), BF16) for w in ws],
        compiler_params=_cparams("arbitrary", "arbitrary"),
        name=name,
    )(*xs, *ws, *[a for _, a in extras])


def _ep_plain(accs, extras):
    return accs[0]


def _ep_scaled(accs, extras):
    return accs[0] * SCALE


def _ep_sigmoid(accs, extras):
    return _sigmoid(accs[0])


def _ep_gated_pair(accs, extras):
    return extras[0].astype(F32) * accs[0] + extras[1].astype(F32) * accs[1]


def _ep_groupnorm(accs, extras, post_scale):
    acc = accs[0]
    gain, flag = extras
    outs = []
    for c in range(acc.shape[1] // LANES):
        sl = slice(c * LANES, (c + 1) * LANES)
        x = acc[:, sl]
        outs.append(jnp.where(flag[:, sl] > 0.0, _rms(x, gain[:, sl]), x) * post_scale)
    return jnp.concatenate(outs, axis=1)


def _upper_ones(n):
    j = lax.broadcasted_iota(I32, (n, n), 0)
    s = lax.broadcasted_iota(I32, (n, n), 1)
    return jnp.where(j >= s, 1.0, 0.0).astype(BF16)


def _sb_block(q, k, v, ones_u, later, mask):
    z = _dot_t(q, k)
    sp = _softplus(z)
    if mask is not None:
        sp = jnp.where(mask, sp, 0.0)
    csum = _dot(sp.astype(BF16), ones_u)
    a = jnp.exp(z - csum - later)
    if mask is not None:
        a = jnp.where(mask, a, 0.0)
    return _dot(a.astype(BF16), v), later + csum[:, 0:1]


SB_HEADS_PER_STEP = 2


def _sb_prompt_kernel(q_ref, k_ref, v_ref, o_ref, *, tq):
    qi = pl.program_id(2)
    ones_u = _upper_ones(tq)
    row = lax.broadcasted_iota(I32, (tq, tq), 0)
    col = lax.broadcasted_iota(I32, (tq, tq), 1)
    heads = [slice(h * HEAD_DIM, (h + 1) * HEAD_DIM) for h in range(SB_HEADS_PER_STEP)]
    qs = [q_ref[:, sl] for sl in heads]

    def kv(kj, sl):
        start = pl.multiple_of(kj * tq, tq)
        return k_ref[pl.ds(start, tq), sl].astype(BF16), v_ref[pl.ds(start, tq), sl].astype(BF16)

    carry = []
    for q, sl in zip(qs, heads):
        k, v = kv(qi, sl)
        carry.extend(_sb_block(q, k, v, ones_u, jnp.zeros((tq, 1), F32), col < row))

    def body(step, carry):
        new = []
        for n, (q, sl) in enumerate(zip(qs, heads)):
            k, v = kv(qi - 1 - step, sl)
            out, later = _sb_block(q, k, v, ones_u, carry[2 * n + 1], None)
            new.extend((carry[2 * n] + out, later))
        return tuple(new)

    carry = lax.fori_loop(0, qi, body, tuple(carry))
    for n, sl in enumerate(heads):
        o_ref[:, sl] = carry[2 * n].astype(o_ref.dtype)


def _sb_prompt(q_sb, kv_sb, nb, t):
    tq = 256
    nq = t // tq
    width = SB_HEADS_PER_STEP * HEAD_DIM
    n_hp = SB_HEADS // SB_HEADS_PER_STEP
    return pl.pallas_call(
        functools.partial(_sb_prompt_kernel, tq=tq),
        out_shape=jax.ShapeDtypeStruct(q_sb.shape, BF16),
        grid=(nb, n_hp, nq),
        in_specs=[pl.BlockSpec((tq, width), lambda b, h, i: (b * nq + i, h)),
                  pl.BlockSpec((t, width), lambda b, h, i: (b, h)),
                  pl.BlockSpec((t, width), lambda b, h, i: (b, n_hp + h))],
        out_specs=pl.BlockSpec((tq, width), lambda b, h, i: (b * nq + i, h)),
        compiler_params=_cparams("arbitrary", "arbitrary", "arbitrary"),
        name="sb_prompt",
    )(q_sb, kv_sb, kv_sb)


def _later_ones(n):
    s = lax.broadcasted_iota(I32, (n, n), 0)
    j = lax.broadcasted_iota(I32, (n, n), 1)
    return jnp.where(j >= s, 1.0, 0.0).astype(BF16)


def _sb_sample_kernel(pt_ref, qr_ref, new_ref, *refs, n_pg, dec):
    pages = refs[:n_pg]
    o_ref = refs[n_pg]
    acc_ref, later_ref, rows_ref = refs[n_pg + 1:]
    j = pl.program_id(1)
    cols = SB_HEADS * dec
    rows_per_tok = 2 * SB_HEADS
    n_tok = n_pg * PAGE
    pitch = n_tok + SUBLANES

    def process(k_heads, v_heads, n_keys, mask):
        pair = lambda xs, h: jnp.concatenate([xs[h], xs[h + 1]], axis=1)
        z = _dot(pair(k_heads, 0), qr_ref[0, 0])
        for h in range(2, SB_HEADS, 2):
            z = z + _dot(pair(k_heads, h), qr_ref[0, h // 2])
        sp = _softplus(z)
        if mask is not None:
            sp = jnp.where(mask, sp, 0.0)
        sp_hi, sp_lo = _split(sp)
        ones_l = _later_ones(n_keys)
        csum = _dot(ones_l, sp_hi) + _dot(ones_l, sp_lo)
        a = jnp.exp(z - csum - later_ref[...])
        if mask is not None:
            a = jnp.where(mask, a, 0.0)
        a = a.T.astype(BF16)
        for h in range(0, SB_HEADS, 2):
            both = _dot(a[h * dec:(h + 2) * dec], pair(v_heads, h))
            acc_ref[h * dec:(h + 1) * dec, :] += both[:dec, :HEAD_DIM]
            acc_ref[(h + 1) * dec:(h + 2) * dec, :] += both[dec:, HEAD_DIM:]
        later_ref[...] += csum[0:1, :]

    @pl.when(j == 0)
    def _():
        acc_ref[...] = jnp.zeros_like(acc_ref)
        later_ref[...] = jnp.zeros_like(later_ref)
        pad = jnp.zeros((PAGE - dec, HEAD_DIM), F32)
        new = new_ref[...]
        head = lambda c: jnp.concatenate([new[:, c * HEAD_DIM:(c + 1) * HEAD_DIM], pad], axis=0).astype(BF16)
        s = lax.broadcasted_iota(I32, (PAGE, cols), 0)
        t = lax.broadcasted_iota(I32, (PAGE, cols), 1) % dec
        process([head(h) for h in range(SB_HEADS)], [head(SB_HEADS + h) for h in range(SB_HEADS)], PAGE, s < t)

    for p in range(n_pg):
        page = pages[n_pg - 1 - p].at[0]
        for tok in range(PAGE):
            for c0 in range(0, rows_per_tok, SUBLANES):
                rows_ref[pl.ds(c0 * pitch + p * PAGE + tok, SUBLANES, stride=pitch), :] = (
                    page[pl.ds(tok * rows_per_tok + c0, SUBLANES), :])

    def rows_of(c):
        return rows_ref[pl.ds(c * pitch, n_tok), :].astype(BF16)

    process([rows_of(h) for h in range(SB_HEADS)], [rows_of(SB_HEADS + h) for h in range(SB_HEADS)],
            n_tok, None)

    @pl.when(j == pl.num_programs(1) - 1)
    def _():
        for h in range(SB_HEADS):
            o_ref[:, h * HEAD_DIM:(h + 1) * HEAD_DIM] = acc_ref[h * dec:(h + 1) * dec, :].astype(o_ref.dtype)


def _sb_sample(q_rhs, kv_new, cache, page_table, dec):
    nb, n_pages = page_table.shape
    n_pg = _pick_tile(n_pages, 4)
    width = SB_HEADS * HEAD_DIM
    cols = SB_HEADS * dec
    assert cols == LANES
    page_rows = cache.shape[1]

    def page_spec(p):
        return pl.BlockSpec((1, page_rows, HEAD_DIM),
                            lambda b, j, pt: (pt[b, n_pages - 1 - (j * n_pg + p)], 0, 0))

    grid_spec = pltpu.PrefetchScalarGridSpec(
        num_scalar_prefetch=1,
        grid=(nb, n_pages // n_pg),
        in_specs=[pl.BlockSpec((1, SB_HEADS // 2, 2 * HEAD_DIM, cols), lambda b, j, pt: (b, 0, 0, 0)),
                  pl.BlockSpec((dec, 2 * width), lambda b, j, pt: (b, 0))]
        + [page_spec(p) for p in range(n_pg)],
        out_specs=pl.BlockSpec((dec, width), lambda b, j, pt: (b, 0)),
        scratch_shapes=[pltpu.VMEM((cols, HEAD_DIM), F32), pltpu.VMEM((1, cols), F32),
                        pltpu.VMEM((2 * SB_HEADS * (n_pg * PAGE + SUBLANES), HEAD_DIM), F32)],
    )
    return pl.pallas_call(
        functools.partial(_sb_sample_kernel, n_pg=n_pg, dec=dec),
        out_shape=jax.ShapeDtypeStruct((nb * dec, width), BF16),
        grid_spec=grid_spec,
        compiler_params=_cparams("arbitrary", "arbitrary"),
        name="sb_sample",
    )(page_table, q_rhs, kv_new, *([cache] * n_pg))


def _compress_kernel(pt_ref, *refs, n_pg):
    pages = refs[:n_pg]
    w_refs = refs[n_pg:n_pg + 2]
    out_refs = refs[n_pg + 2:n_pg + 4]
    cpp = PAGE // CMP_STRIDE
    half = n_pg * cpp
    rows_per_tok = 4 * NSA_G
    for slot in range(2):
        acc = jnp.zeros((2 * half, 2 * CMP_HIDDEN), F32)
        for r in range(CMP_STRIDE):
            pieces = []
            for g in range(NSA_G):
                first = r * rows_per_tok + slot * NSA_G + g
                for p in range(n_pg):
                    pieces.append(pages[p].at[0][pl.ds(first, cpp, stride=CMP_STRIDE * rows_per_tok), :])
            x = jnp.concatenate(pieces, axis=0).astype(BF16)
            acc = acc + _dot(x, w_refs[slot][r])
        for g in range(NSA_G):
            out_refs[slot][0, g] = acc[g * half:(g + 1) * half]


def _compress(rows, table, w_k, w_v, n_pg):
    nb, n_pages = table.shape
    assert n_pages % n_pg == 0
    cpp = PAGE // CMP_STRIDE
    page_rows = rows.shape[1]

    def page_spec(p):
        return pl.BlockSpec((1, page_rows, HEAD_DIM), lambda b, j, pt: (pt[b, j * n_pg + p], 0, 0))

    w_spec = pl.BlockSpec((CMP_STRIDE, HEAD_DIM, 2 * CMP_HIDDEN), lambda b, j, pt: (0, 0, 0))
    out_spec = pl.BlockSpec((1, NSA_G, n_pg * cpp, 2 * CMP_HIDDEN), lambda b, j, pt: (b, 0, j, 0))
    out_sds = jax.ShapeDtypeStruct((nb, NSA_G, n_pages * cpp, 2 * CMP_HIDDEN), F32)
    grid_spec = pltpu.PrefetchScalarGridSpec(
        num_scalar_prefetch=1,
        grid=(nb, n_pages // n_pg),
        in_specs=[page_spec(p) for p in range(n_pg)] + [w_spec, w_spec],
        out_specs=[out_spec, out_spec],
    )
    return pl.pallas_call(
        functools.partial(_compress_kernel, n_pg=n_pg),
        out_shape=[out_sds, out_sds],
        grid_spec=grid_spec,
        compiler_params=_cparams("arbitrary", "arbitrary"),
        name="nsa_compress",
    )(table, *([rows] * n_pg), w_k, w_v)


def _gelu_tanh(x):
    return 0.5 * x * (1.0 + jnp.tanh(0.7978845608028654 * (x + 0.044715 * x * x * x)))


def _cmp_finish_kernel(*refs, nc_pad, n_tail):
    if n_tail:
        hk_ref, hv_ref, tk_ref, tv_ref = refs[:4]
        refs = refs[4:]
    else:
        hk_ref, hv_ref = refs[:2]
        tk_ref = tv_ref = None
        refs = refs[2:]
    bk_ref, bv_ref, w2k_ref, w2v_ref, g_ref, kc_ref, vc_ref = refs
    row = lax.broadcasted_iota(I32, (nc_pad, CMP_HIDDEN), 0)

    def chunks(h_ref, t_ref):
        parts = [h_ref[0, 0]]
        if t_ref is not None:
            tail = t_ref[0, 0]
            keep = lax.broadcasted_iota(I32, tail.shape, 0) < n_tail
            parts.append(jnp.where(keep, tail, 0.0))
        n = sum(p.shape[0] for p in parts)
        if nc_pad > n:
            parts.append(jnp.zeros((nc_pad - n, 2 * CMP_HIDDEN), F32))
        return jnp.concatenate(parts, axis=0) if len(parts) > 1 else parts[0]

    def branch(h_ref, t_ref, b_ref, w2_ref):
        h = chunks(h_ref, t_ref)
        nxt = pltpu.roll(h[:, CMP_HIDDEN:], nc_pad - 1, 0)
        nxt = jnp.where(row == nc_pad - 1, 0.0, nxt)
        hid = _gelu_tanh(h[:, :CMP_HIDDEN] + nxt + b_ref[0:1, :])
        return _dot(hid.astype(BF16), w2_ref[...].astype(BF16))

    kc_ref[0, 0] = _rms(branch(hk_ref, tk_ref, bk_ref, w2k_ref), g_ref[...])
    vc_ref[0, 0] = branch(hv_ref, tv_ref, bv_ref, w2v_ref)


def _pos_bias_kernel(posk_ref, posv_ref, w1k_ref, w1v_ref, bk_ref, bv_ref):
    bk_ref[...] = _dot3(posk_ref[...], w1k_ref[...])
    bv_ref[...] = _dot3(posv_ref[...], w1v_ref[...])


def _pos_bias(pos_k, pos_v, w1k, w1v):
    pos_k8 = jnp.tile(pos_k.reshape(1, -1), (SUBLANES, 1))
    pos_v8 = jnp.tile(pos_v.reshape(1, -1), (SUBLANES, 1))
    sds = jax.ShapeDtypeStruct((SUBLANES, CMP_HIDDEN), F32)
    return pl.pallas_call(_pos_bias_kernel, out_shape=[sds, sds],
                          compiler_params=pltpu.CompilerParams(vmem_limit_bytes=VMEM_LIMIT),
                          name="nsa_pos_bias")(pos_k8, pos_v8, w1k, w1v)


def _cmp_finish(hk, hv, tails, n_tail, nc_pad, bias_k, bias_v, w2k, w2v, kc_g):
    nb, _, n_chunks, _ = hk.shape
    assert n_chunks % SUBLANES == 0 and nc_pad % SUBLANES == 0
    h_spec = pl.BlockSpec((1, 1, n_chunks, 2 * CMP_HIDDEN), lambda b, g: (b, g, 0, 0))
    t_specs = [pl.BlockSpec((1, 1) + t.shape[2:], lambda b, g: (b, g, 0, 0)) for t in tails]
    o_spec = pl.BlockSpec((1, 1, nc_pad, HEAD_DIM), lambda b, g: (b, g, 0, 0))
    o_sds = jax.ShapeDtypeStruct((nb, NSA_G, nc_pad, HEAD_DIM), F32)

    def full(a):
        return pl.BlockSpec(a.shape, lambda b, g: (0,) * a.ndim)

    kc_g = kc_g.reshape(1, HEAD_DIM)
    args = (bias_k, bias_v, w2k, w2v, kc_g)
    return pl.pallas_call(
        functools.partial(_cmp_finish_kernel, nc_pad=nc_pad, n_tail=n_tail if tails else 0),
        out_shape=[o_sds, o_sds],
        grid=(nb, NSA_G),
        in_specs=[h_spec, h_spec] + t_specs + [full(a) for a in args],
        out_specs=[o_spec, o_spec],
        compiler_params=_cparams("arbitrary", "arbitrary"),
        name="nsa_cmp_finish",
    )(hk, hv, *tails, *args)


def _cmp_select_kernel(slopes_ref, q_ref, kc_ref, vc_ref, oc_ref, sel_ref, *, tr, nc_pad, ns_pad, ns, base):
    g = pl.program_id(1)
    i = pl.program_id(2)
    q = q_ref[...]
    kc = kc_ref[0, 0].astype(BF16)
    vc = vc_ref[0, 0].astype(BF16)
    pos = base + i * tr + lax.broadcasted_iota(I32, (tr, 1), 0)
    end = lax.broadcasted_iota(I32, (1, nc_pad), 1) * CMP_STRIDE + (CMP_BLOCK - 1)
    dist = (pos - end).astype(F32)
    valid = dist >= 0.0
    score = jnp.zeros((tr, nc_pad), F32)
    for r in range(NSA_REP):
        sl = slice(r * HEAD_DIM, (r + 1) * HEAD_DIM)
        s = _dot_t(q[:, sl], kc) - slopes_ref[g, r] * dist
        p = _masked_softmax(s, valid)
        oc_ref[:, sl] = _dot(p.astype(BF16), vc)
        score = score + p
    n_id = lax.broadcasted_iota(I32, (nc_pad, ns_pad), 0)
    j_id = lax.broadcasted_iota(I32, (nc_pad, ns_pad), 1)
    group = jnp.where((n_id // SEL_RATIO) == j_id, 1.0, 0.0).astype(BF16)
    sc = _dot2_exact_rhs(score, group)
    j = lax.broadcasted_iota(I32, (tr, ns_pad), 1)
    cur = pos // SEL_BLOCK
    forced = (j == 0) | (j == cur) | (j == cur - 1)
    sc = jnp.where(j > cur, NEG_INF, jnp.where(forced, FORCE_SCORE, sc))
    sc = jnp.where(j < ns, sc, LOWEST)
    rank = jnp.zeros((tr, ns_pad), F32)
    for c in range(ns):
        other = sc[:, c:c + 1]
        ahead = (other > sc) | ((other == sc) & (c < j))
        rank = rank + jnp.where(ahead, 1.0, 0.0)
    sel_ref[0, 0] = jnp.where((rank < float(min(SEL_TOPK, ns))) & (j < ns), 1.0, 0.0)


def _cmp_select(q_n, kc, vc, slopes, nb, t, tr, ns, base):
    nc_pad = kc.shape[2]
    ns_pad = -(-ns // LANES) * LANES
    nt = t // tr
    width = NSA_REP * HEAD_DIM
    kv_spec = pl.BlockSpec((1, 1, nc_pad, HEAD_DIM), lambda b, g, i: (b, g, 0, 0))
    return pl.pallas_call(
        functools.partial(_cmp_select_kernel, tr=tr, nc_pad=nc_pad, ns_pad=ns_pad, ns=ns, base=base),
        out_shape=[jax.ShapeDtypeStruct(q_n.shape, F32),
                   jax.ShapeDtypeStruct((nb, NSA_G, t, ns_pad), F32)],
        grid=(nb, NSA_G, nt),
        in_specs=[pl.BlockSpec(memory_space=pltpu.SMEM),
                  pl.BlockSpec((tr, width), lambda b, g, i: (b * nt + i, g)), kv_spec, kv_spec],
        out_specs=[pl.BlockSpec((tr, width), lambda b, g, i: (b * nt + i, g)),
                   pl.BlockSpec((1, 1, tr, ns_pad), lambda b, g, i: (b, g, i, 0))],
        compiler_params=_cparams("arbitrary", "arbitrary", "arbitrary"),
        name="nsa_cmp_select",
    )(slopes, q_n, kc, vc)


def _block_expand(n_blocks, key_pos):
    j = lax.broadcasted_iota(I32, (n_blocks, key_pos.shape[1]), 0)
    return jnp.where(j == key_pos // SEL_BLOCK, 1.0, 0.0).astype(BF16)


def _biased_attention(qh, k, v, slope, neg_dist):
    s = _dot_t(qh, k) + slope * neg_dist
    e = jnp.exp(s - jnp.max(s, axis=-1, keepdims=True))
    return _dot(e.astype(BF16), v) / jnp.sum(e, axis=-1, keepdims=True)


def _nsa_prompt_kernel(slopes_ref, q_ref, sk_ref, sv_ref, wk_ref, wv_ref, sel_ref, oc_ref, gate_ref, o_ref,
                       part_ref, *, tq, t, n_cls):
    g = pl.program_id(1)
    i = pl.program_id(2)
    t0 = i * tq
    span = WINDOW + tq
    qpos = t0 + lax.broadcasted_iota(I32, (tq, 1), 0)
    gates = gate_ref[0]
    heads = [slice(r * HEAD_DIM, (r + 1) * HEAD_DIM) for r in range(NSA_REP)]

    wstart = pl.multiple_of(jnp.clip(t0 - WINDOW, 0, t - span), tq)
    wk = wk_ref[pl.ds(wstart, span), :].astype(BF16)
    wv = wv_ref[pl.ds(wstart, span), :].astype(BF16)
    dist_w = (qpos - (wstart + lax.broadcasted_iota(I32, (1, span), 1))).astype(F32)
    nd_w = jnp.where((dist_w >= 0.0) & (dist_w < float(WINDOW)), -dist_w, NEG_INF)
    for r, sl in enumerate(heads):
        o_w = _biased_attention(q_ref[:, sl], wk, wv, slopes_ref[g, r], nd_w)
        part_ref[:, sl] = gates[:, 3 * r:3 * r + 1] * oc_ref[:, sl] + gates[:, 3 * r + 2:3 * r + 3] * o_w

    def selected(n_keys):
        sk = sk_ref[pl.ds(0, n_keys), :].astype(BF16)
        sv = sv_ref[pl.ds(0, n_keys), :].astype(BF16)
        kpos = lax.broadcasted_iota(I32, (1, n_keys), 1)
        dist = (qpos - kpos).astype(F32)
        sel = sel_ref[0, 0].astype(BF16)
        picked = _dot(sel, _block_expand(sel.shape[1], kpos))
        nd = jnp.where((dist >= 0.0) & (picked > 0.5), -dist, NEG_INF)
        for r, sl in enumerate(heads):
            o_s = _biased_attention(q_ref[:, sl], sk, sv, slopes_ref[g, r], nd)
            o_ref[:, sl] = (part_ref[:, sl] + gates[:, 3 * r + 1:3 * r + 2] * o_s).astype(o_ref.dtype)

    chunk = t // n_cls
    cls = ((i + 1) * tq - 1) // chunk
    for c in range(n_cls):
        pl.when(cls == c)(functools.partial(selected, (c + 1) * chunk))


def _nsa_prompt(q_n, nsa_kv, win_kv, sel, o_c, gates, slopes, nb, t):
    tq = 128
    assert t >= WINDOW + tq and t % tq == 0
    n_cls = 4 if t % (4 * tq) == 0 else 1
    nt = t // tq
    width = NSA_REP * HEAD_DIM
    ns_pad = sel.shape[-1]
    q_spec = pl.BlockSpec((tq, width), lambda b, g, i: (b * nt + i, g))

    def kv_spec(slot):
        return pl.BlockSpec((t, HEAD_DIM), lambda b, g, i: (b, slot * NSA_G + g))

    return pl.pallas_call(
        functools.partial(_nsa_prompt_kernel, tq=tq, t=t, n_cls=n_cls),
        out_shape=jax.ShapeDtypeStruct(q_n.shape, BF16),
        scratch_shapes=[pltpu.VMEM((tq, width), F32)],
        grid=(nb, NSA_G, nt),
        in_specs=[pl.BlockSpec(memory_space=pltpu.SMEM), q_spec,
                  kv_spec(2), kv_spec(3), kv_spec(0), kv_spec(1),
                  pl.BlockSpec((1, 1, tq, ns_pad), lambda b, g, i: (b, g, i, 0)),
                  q_spec,
                  pl.BlockSpec((1, tq, LANES), lambda b, g, i: (g, b * nt + i, 0))],
        out_specs=q_spec,
        compiler_params=_cparams("arbitrary", "arbitrary", "arbitrary"),
        name="nsa_prompt",
    )(slopes, q_n, nsa_kv, nsa_kv, win_kv, win_kv, sel, o_c, gates)


def _nsa_sample_kernel(pt_ref, slopes_ref, q_ref, sel_ref, oc_ref, gate_ref, new_ref, wnew_ref, wcache_ref,
                       *refs, n_pg, dec, past):
    pages = refs[:n_pg]
    o_ref = refs[n_pg]
    m_ref, l_ref, acc_ref = refs[n_pg + 1:]
    j = pl.program_id(1)
    rows = NSA_REP * dec
    ns_pad = sel_ref.shape[-1]
    row = lax.broadcasted_iota(I32, (rows, 1), 0)
    qpos = past + row % dec

    @pl.when(j == 0)
    def _():
        m_ref[...] = jnp.full_like(m_ref, NEG_INF)
        l_ref[...] = jnp.zeros_like(l_ref)
        acc_ref[...] = jnp.zeros_like(acc_ref)

    def stack_heads(x, g):
        return jnp.concatenate(
            [x[:, (g * NSA_REP + r) * HEAD_DIM:(g * NSA_REP + r + 1) * HEAD_DIM] for r in range(NSA_REP)], axis=0)

    def slope_col(g):
        col = jnp.zeros((rows, 1), F32)
        for r in range(NSA_REP):
            col = jnp.where(row // dec == r, slopes_ref[g, r], col)
        return col

    def online(g, s, mask, v):
        s = jnp.where(mask, s, NEG_INF)
        m_old = m_ref[g]
        m_new = jnp.maximum(m_old, jnp.max(s, axis=-1, keepdims=True))
        alpha = jnp.exp(m_old - m_new)
        e = jnp.where(mask, jnp.exp(s - m_new), 0.0)
        l_ref[g] = alpha * l_ref[g] + jnp.sum(e, axis=-1, keepdims=True)
        acc_ref[g] = alpha * acc_ref[g] + _dot(e.astype(BF16), v)
        m_ref[g] = m_new

    def sel_rows(g):
        return jnp.concatenate([sel_ref[0, g]] * NSA_REP, axis=0).astype(BF16)

    def attend_selected(g, qg, k, v, kpos, expand):
        dist = (qpos - kpos).astype(F32)
        picked = _dot(sel_rows(g), expand)
        s = _dot_t(qg, k) - slope_col(g) * dist
        online(g, s, (dist >= 0.0) & (picked > 0.5), v)

    q = q_ref[...]
    n_keys = n_pg * PAGE
    kpos_pages = j * n_keys + lax.broadcasted_iota(I32, (1, n_keys), 1)
    rows_per_tok = 4 * NSA_G

    def page_rows(c):
        return jnp.concatenate([pages[p].at[0][pl.ds(c, PAGE, stride=rows_per_tok), :] for p in range(n_pg)],
                               axis=0).astype(BF16)

    expand_pages = _block_expand(ns_pad, kpos_pages)
    for g in range(NSA_G):
        attend_selected(g, stack_heads(q, g), page_rows(2 * NSA_G + g), page_rows(3 * NSA_G + g), kpos_pages,
                        expand_pages)

    @pl.when(j == pl.num_programs(1) - 1)
    def _():
        pad = jnp.zeros((PAGE - dec, HEAD_DIM), F32)
        kpos_new = past + lax.broadcasted_iota(I32, (1, PAGE), 1)
        win_rows = 2 * NSA_G
        nbuf = wcache_ref.shape[1] // win_rows
        kpos_w = jnp.concatenate([past - nbuf + lax.broadcasted_iota(I32, (1, nbuf), 1), kpos_new], axis=1)
        dist_w = (qpos - kpos_w).astype(F32)
        key_ok = jnp.concatenate([jnp.ones((1, nbuf), F32),
                                  jnp.where(lax.broadcasted_iota(I32, (1, PAGE), 1) < dec, 1.0, 0.0)], axis=1)
        mask_w = (dist_w >= 0.0) & (dist_w < float(WINDOW)) & (kpos_w >= 0) & (key_ok > 0.5)
        expand_new = _block_expand(ns_pad, kpos_new)
        for g in range(NSA_G):
            qg = stack_heads(q, g)
            lane_k = (2 * NSA_G + g) * HEAD_DIM
            lane_v = (3 * NSA_G + g) * HEAD_DIM
            k_new = jnp.concatenate([new_ref[:, lane_k:lane_k + HEAD_DIM], pad], axis=0).astype(BF16)
            v_new = jnp.concatenate([new_ref[:, lane_v:lane_v + HEAD_DIM], pad], axis=0).astype(BF16)
            attend_selected(g, qg, k_new, v_new, kpos_new, expand_new)
            o_s = acc_ref[g] / l_ref[g]
            lw_k = g * HEAD_DIM
            lw_v = (NSA_G + g) * HEAD_DIM
            wk = jnp.concatenate([wcache_ref.at[0][pl.ds(g, nbuf, stride=win_rows), :],
                                  wnew_ref[:, lw_k:lw_k + HEAD_DIM], pad], axis=0).astype(BF16)
            wv = jnp.concatenate([wcache_ref.at[0][pl.ds(NSA_G + g, nbuf, stride=win_rows), :],
                                  wnew_ref[:, lw_v:lw_v + HEAD_DIM], pad], axis=0).astype(BF16)
            p_w = _masked_softmax(_dot_t(qg, wk) - slope_col(g) * dist_w, mask_w)
            o_w = _dot(p_w.astype(BF16), wv)
            gates = gate_ref[g]
            for r in range(NSA_REP):
                rs = slice(r * dec, (r + 1) * dec)
                sl = slice((g * NSA_REP + r) * HEAD_DIM, (g * NSA_REP + r + 1) * HEAD_DIM)
                out = (gates[:, 3 * r:3 * r + 1] * oc_ref[:, sl] + gates[:, 3 * r + 1:3 * r + 2] * o_s[rs]
                       + gates[:, 3 * r + 2:3 * r + 3] * o_w[rs])
                o_ref[:, sl] = out.astype(o_ref.dtype)


def _nsa_sample(q_n, sel, o_c, gates, nsa_new, win_new, win_cache, cache, page_table, slopes, dec):
    nb, n_pages = page_table.shape
    n_pg = _pick_tile(n_pages, 8)
    past = n_pages * PAGE
    rows = NSA_REP * dec
    ns_pad = sel.shape[-1]
    lanes = nsa_new.shape[-1]
    wl = win_new.shape[-1]
    page_rows = cache.shape[1]

    def page_spec(p):
        return pl.BlockSpec((1, page_rows, HEAD_DIM), lambda b, j, pt: (pt[b, j * n_pg + p], 0, 0))

    tok = lambda w: pl.BlockSpec((dec, w), lambda b, j, pt: (b, 0))
    grid_spec = pltpu.PrefetchScalarGridSpec(
        num_scalar_prefetch=1,
        grid=(nb, n_pages // n_pg),
        in_specs=[pl.BlockSpec(memory_space=pltpu.SMEM),
                  tok(NSA_HEADS * HEAD_DIM),
                  pl.BlockSpec((1, NSA_G, dec, ns_pad), lambda b, j, pt: (b, 0, 0, 0)),
                  tok(NSA_HEADS * HEAD_DIM),
                  pl.BlockSpec((NSA_G, dec, LANES), lambda b, j, pt: (0, b, 0)),
                  tok(lanes), tok(wl),
                  pl.BlockSpec((1, win_cache.shape[1], HEAD_DIM), lambda b, j, pt: (b, 0, 0))]
        + [page_spec(p) for p in range(n_pg)],
        out_specs=tok(NSA_HEADS * HEAD_DIM),
        scratch_shapes=[pltpu.VMEM((NSA_G, rows, 1), F32), pltpu.VMEM((NSA_G, rows, 1), F32),
                        pltpu.VMEM((NSA_G, rows, HEAD_DIM), F32)],
    )
    return pl.pallas_call(
        functools.partial(_nsa_sample_kernel, n_pg=n_pg, dec=dec, past=past),
        out_shape=jax.ShapeDtypeStruct(q_n.shape, BF16),
        grid_spec=grid_spec,
        compiler_params=_cparams("arbitrary", "arbitrary"),
        name="nsa_sample",
    )(page_table, slopes, q_n, sel, o_c, gates, nsa_new, win_new, win_cache, *([cache] * n_pg))


def _resid_route_kernel(x_ref, mix_ref, gate_ref, g_ref, scale_ref, shift_ref, wr_hi_ref, wr_lo_ref, br_ref,
                        x1_ref, h_ref, rid_ref, rw_ref):
    x1 = x_ref[...] + _as2d(gate_ref[...]) * mix_ref[...]
    x1_ref[...] = x1
    h = _rms(x1, g_ref[...]) * (1.0 + _as2d(scale_ref[...])) + _as2d(shift_ref[...])
    h_ref[...] = h
    h_hi, h_lo = _split(h)
    logits = _dot(h_hi, wr_hi_ref[...]) + _dot(h_lo, wr_hi_ref[...]) + _dot(h_hi, wr_lo_ref[...]) + br_ref[...]
    tm = logits.shape[0]
    lane = lax.broadcasted_iota(I32, (tm, LANES), 1).astype(F32)
    far = float(LANES)

    def first_lane(cond):
        return jnp.min(jnp.where(cond, lane, far), axis=-1, keepdims=True)

    in_groups = lane < float(MOE_GROUPS)
    lg = jnp.where(in_groups, logits, LOWEST)
    g_max = jnp.max(lg, axis=-1, keepdims=True)
    g_sum = jnp.sum(jnp.where(in_groups, jnp.exp(lg - g_max), 0.0), axis=-1, keepdims=True)
    g_top = 1.0 / g_sum
    g_idx = first_lane(in_groups & (lg == g_max))
    lo = float(MOE_GROUPS) + float(EPG) * g_idx
    in_experts = (lane >= lo) & (lane < lo + float(EPG))
    le = jnp.where(in_experts, logits, LOWEST)
    e_max = jnp.max(le, axis=-1, keepdims=True)
    e_sum = jnp.sum(jnp.where(in_experts, jnp.exp(le - e_max), 0.0), axis=-1, keepdims=True)
    i1 = first_lane(in_experts & (le == e_max))
    le2 = jnp.where(lane == i1, LOWEST, le)
    e_max2 = jnp.max(le2, axis=-1, keepdims=True)
    i2 = first_lane(in_experts & (lane != i1) & (le2 == e_max2))
    p1 = 1.0 / e_sum
    p2 = jnp.exp(e_max2 - e_max) / e_sum
    w1 = p1 / (p1 + p2) * g_top
    w2 = p2 / (p1 + p2) * g_top
    first = lane == 0.0
    second = lane == 1.0
    ids = jnp.where(first, i1 - float(MOE_GROUPS), jnp.where(second, i2 - float(MOE_GROUPS), 0.0))
    rid_ref[...] = ids.astype(I32)
    rw_ref[...] = jnp.where(first, w1, jnp.where(second, w2, 0.0))


def _resid_route(x, mix, gain, mod, per_token, nb, wr_hi, wr_lo, br, tm):
    m, d = x.shape
    nt = m // nb // tm
    row = pl.BlockSpec((tm, d), lambda b, i: (b * nt + i, 0))
    lane_row = pl.BlockSpec((tm, LANES), lambda b, i: (b * nt + i, 0))
    const = lambda a: pl.BlockSpec(a.shape, lambda b, i: (0, 0))
    gain = gain.reshape(1, d)
    return pl.pallas_call(
        _resid_route_kernel,
        out_shape=[jax.ShapeDtypeStruct((m, d), F32), jax.ShapeDtypeStruct((m, d), F32),
                   jax.ShapeDtypeStruct((m, LANES), I32), jax.ShapeDtypeStruct((m, LANES), F32)],
        grid=(nb, nt),
        in_specs=[row, row, _mod_spec(per_token, tm, d, 2, nt), const(gain),
                  _mod_spec(per_token, tm, d, 4, nt), _mod_spec(per_token, tm, d, 3, nt),
                  const(wr_hi), const(wr_lo), const(br)],
        out_specs=[row, row, lane_row, lane_row],
        compiler_params=_cparams("arbitrary", "arbitrary"),
        name="resid_route",
    )(x, mix, mod, gain, mod, mod, wr_hi, wr_lo, br)


def _moe_kernel(texp_ref, nused_ref, first_ref, src_ref, dst_ref, h_hbm, roww_ref, wg_ref, wu_ref, wd_ref, y_hbm,
                xbuf, xbf, wg_bf, wu_bf, wd_bf, sem_in, sem_out, *, tm):
    i = pl.program_id(0)
    f = pl.program_id(1)
    n_used = nused_ref[0]
    last_f = pl.num_programs(1) - 1
    slot = i % 2

    def gather_copy(r, src_row, s):
        return pltpu.make_async_copy(h_hbm.at[pl.ds(src_row, 1)], xbuf.at[s, pl.ds(r, 1)], sem_in.at[s])

    def scatter_copy(r, dst_row, s):
        return pltpu.make_async_copy(xbuf.at[s, pl.ds(r, 1)], y_hbm.at[pl.ds(dst_row, 1)], sem_out)

    def each_row(fn):
        def body(r, c):
            fn(r)
            return c

        lax.fori_loop(0, tm, body, 0, unroll=8)

    def gather_start(tile, s):
        each_row(lambda r: gather_copy(r, src_ref[tile * tm + r], s).start())

    def gather_wait(s):
        each_row(lambda r: gather_copy(r, 0, s).wait())

    def scatter(tile, start):
        s = tile % 2

        def one(r):
            dst = dst_ref[tile * tm + r]

            @pl.when(dst >= 0)
            def _():
                if start:
                    scatter_copy(r, dst, s).start()
                else:
                    scatter_copy(r, 0, s).wait()

        each_row(one)

    @pl.when(i < n_used)
    def _():
        @pl.when(first_ref[i] == 1)
        def _():
            wg_bf[f] = wg_ref[0].astype(BF16)
            wu_bf[f] = wu_ref[0].astype(BF16)
            wd_bf[f] = wd_ref[0].astype(BF16)

        @pl.when(f == 0)
        def _():
            @pl.when(i == 0)
            def _():
                gather_start(0, 0)

            gather_wait(slot)
            xbf[...] = xbuf[slot].astype(BF16)

        @pl.when((f == last_f) & (i + 1 < n_used))
        def _():
            gather_start(i + 1, 1 - slot)

        x = xbf[...]
        gt = _dot(x, wg_bf[f])
        up = _dot(x, wu_bf[f])
        hid = gt * _sigmoid(gt) * up
        part = _dot(hid.astype(BF16), wd_bf[f])

        @pl.when(f == 0)
        def _():
            @pl.when(i > 0)
            def _():
                scatter(i - 1, start=False)

            xbuf[slot] = part

        @pl.when(f == last_f)
        def _():
            xbuf[slot] = (xbuf[slot] + part) * roww_ref[...]
            scatter(i, start=True)

            @pl.when(i == n_used - 1)
            def _():
                scatter(i, start=False)


def _moe(h_all, route_id, route_w, w_gate, w_up, w_down):
    n, d = h_all.shape
    tm = 256
    n_ff = 2
    ff = EXPERT_FF // n_ff
    a = 2 * n
    t_max = (a + N_EXPERTS * (tm - 1)) // tm
    e_flat = route_id.reshape(-1)
    onehot = (e_flat[:, None] == jnp.arange(N_EXPERTS, dtype=I32)[None, :]).astype(I32)
    running = jnp.cumsum(onehot, axis=0)
    counts = running[-1]
    tiles_per = (counts + tm - 1) // tm
    tiles_end = jnp.cumsum(tiles_per)
    n_used = tiles_end[-1]
    seg_start = (tiles_end - tiles_per) * tm
    dest = jnp.sum(onehot * (running - 1 + seg_start[None, :]), axis=1)
    owner = jnp.zeros((t_max * tm,), I32).at[dest].set(jnp.arange(1, a + 1, dtype=I32)) - 1
    valid = owner >= 0
    src_row = jnp.where(valid, owner // 2, 0)
    dst_row = jnp.where(valid, (owner % 2) * n + owner // 2, -1)
    row_w = jnp.where(valid, route_w.reshape(-1)[jnp.maximum(owner, 0)], 0.0).reshape(-1, 1)
    tile = jnp.minimum(jnp.arange(t_max, dtype=I32), n_used - 1)
    tile_e = jnp.minimum(jnp.sum((tiles_end[None, :] <= tile[:, None]).astype(I32), axis=1), N_EXPERTS - 1)

    first = ((tiles_end - tiles_per)[tile_e] == jnp.arange(t_max, dtype=I32)).astype(I32)

    def half(i, f, fi):
        return jnp.where(fi[i] == 1, f, n_ff - 1)

    assert n_ff == 2
    grid_spec = pltpu.PrefetchScalarGridSpec(
        num_scalar_prefetch=5,
        grid=(t_max, n_ff),
        in_specs=[pl.BlockSpec(memory_space=pl.ANY),
                  pl.BlockSpec((tm, 1), lambda i, f, te, nu, fi, sr, ds: (i, 0)),
                  pl.BlockSpec((1, d, ff), lambda i, f, te, nu, fi, sr, ds: (te[i], 0, half(i, f, fi))),
                  pl.BlockSpec((1, d, ff), lambda i, f, te, nu, fi, sr, ds: (te[i], 0, half(i, f, fi))),
                  pl.BlockSpec((1, ff, d), lambda i, f, te, nu, fi, sr, ds: (te[i], half(i, f, fi), 0))],
        out_specs=pl.BlockSpec(memory_space=pl.ANY),
        scratch_shapes=[pltpu.VMEM((2, tm, d), F32), pltpu.VMEM((tm, d), BF16),
                        pltpu.VMEM((n_ff, d, ff), BF16), pltpu.VMEM((n_ff, d, ff), BF16),
                        pltpu.VMEM((n_ff, ff, d), BF16),
                        pltpu.SemaphoreType.DMA((2,)), pltpu.SemaphoreType.DMA(())],
    )
    return pl.pallas_call(
        functools.partial(_moe_kernel, tm=tm),
        out_shape=jax.ShapeDtypeStruct((a, d), F32),
        grid_spec=grid_spec,
        compiler_params=pltpu.CompilerParams(dimension_semantics=("arbitrary", "arbitrary"),
                                             vmem_limit_bytes=MOE_VMEM_LIMIT, has_side_effects=True),
        name="moe",
    )(tile_e, n_used.reshape(1).astype(I32), first, src_row, dst_row, h_all, row_w, w_gate, w_up, w_down)


def _final_kernel(x_ref, gate_ref, ya_ref, yb_ref, o_ref):
    o_ref[...] = x_ref[...] + _as2d(gate_ref[...]) * (ya_ref[...] + yb_ref[...])


def _final(x1, mod, per_token, nb, y, row_off, n_all, tm):
    m, d = x1.shape
    nt = m // nb // tm
    off_a = row_off // tm
    off_b = (n_all + row_off) // tm
    row = pl.BlockSpec((tm, d), lambda b, i: (b * nt + i, 0))
    return pl.pallas_call(
        _final_kernel,
        out_shape=jax.ShapeDtypeStruct((m, d), F32),
        grid=(nb, nt),
        in_specs=[row, _mod_spec(per_token, tm, d, 5, nt),
                  pl.BlockSpec((tm, d), lambda b, i: (off_a + b * nt + i, 0)),
                  pl.BlockSpec((tm, d), lambda b, i: (off_b + b * nt + i, 0))],
        out_specs=row,
        compiler_params=_cparams("arbitrary", "arbitrary"),
        name="final_residual",
    )(x1, mod, y, y)


def _pick_tile(m, pref):
    t = pref
    while m % t:
        t //= 2
    return t


def _project(h, w_in, gains, d):
    m = h.shape[0]
    tm = _pick_tile(m, 1024)
    tn = 512
    sbw = SB_HEADS * HEAD_DIM
    nw = NSA_HEADS * HEAD_DIM
    kvw = NSA_G * HEAD_DIM
    o_q, o_kv, o_qn, o_nkv, o_win, o_gn = 0, sbw, 3 * sbw, 3 * sbw + nw, 3 * sbw + nw + 4 * kvw, 3 * sbw + nw + 6 * kvw
    mm = functools.partial(_matmul, [h], [w_in], tm=tm, tn=tn)
    q_sb = mm(o_q, sbw, [], _ep_scaled, BF16, name="proj_q_sb")
    kv_sb = mm(o_kv, 2 * sbw, [], _ep_plain, F32, name="proj_kv_sb")

    def normed(off, width, gain_cols, flag_cols, post, dtype, name):
        return mm(off, width, [("col", gain_cols.reshape(1, width)), ("col", flag_cols.reshape(1, width))],
                  functools.partial(_ep_groupnorm, post_scale=post), dtype, name=name)

    ones = lambda k: jnp.ones((k * HEAD_DIM,), F32)
    zeros = lambda k: jnp.zeros((k * HEAD_DIM,), F32)
    q_n = normed(o_qn, nw, jnp.tile(gains["q"], NSA_HEADS), ones(NSA_HEADS), SCALE, BF16, "proj_q_nsa")
    nsa_kv = normed(o_nkv, 4 * kvw,
                    jnp.concatenate([ones(2 * NSA_G), jnp.tile(gains["ks"], NSA_G), ones(NSA_G)]),
                    jnp.concatenate([zeros(2 * NSA_G), ones(NSA_G), zeros(NSA_G)]), 1.0, F32, "proj_nsa_kv")
    win_kv = normed(o_win, 2 * kvw, jnp.concatenate([jnp.tile(gains["kw"], NSA_G), ones(NSA_G)]),
                    jnp.concatenate([ones(NSA_G), zeros(NSA_G)]), 1.0, F32, "proj_win_kv")
    g_n = mm(o_gn, tn, [], _ep_sigmoid, F32, name="proj_g_nsa")[:, :3 * NSA_HEADS]
    w_gates = w_in[:, o_gn + 3 * NSA_HEADS:]
    gate_sb = _matmul([h], [w_gates], 0, d, [], _ep_sigmoid, BF16, tm, tn, "proj_gate_sb")
    gate_n = _matmul([h], [w_gates], d, d, [], _ep_sigmoid, BF16, tm, tn, "proj_gate_nsa")
    gates = jnp.pad(g_n.reshape(m, NSA_G, 3 * NSA_REP).transpose(1, 0, 2),
                    ((0, 0), (0, 0), (0, LANES - 3 * NSA_REP)))
    return q_sb, kv_sb, q_n, nsa_kv, win_kv, gates, gate_sb, gate_n


def _compress_weights(w1):
    half = CMP_STRIDE * HEAD_DIM
    a = w1[:half].reshape(CMP_STRIDE, HEAD_DIM, CMP_HIDDEN)
    b = w1[half:].reshape(CMP_STRIDE, HEAD_DIM, CMP_HIDDEN)
    return jnp.concatenate([a, b], axis=-1).astype(BF16)


def _merge_out(o_sb, o_n, gate_sb, gate_n, w):
    m = o_sb.shape[0]
    d = w["w_out"].shape[1]
    tm = _pick_tile(m, 1024)
    mixed = _matmul([o_sb, o_n], [w["w_br_sb"], w["w_br_nsa"]], 0, d, [("tile", gate_sb), ("tile", gate_n)],
                    _ep_gated_pair, BF16, tm, 512, "merge")
    return _matmul([mixed], [w["w_out"]], 0, d, [], _ep_plain, F32, tm, 512, "out_proj")


def _alibi_slopes():
    h = jnp.arange(1, NSA_HEADS + 1, dtype=F32)
    return jnp.exp2(-8.0 * h / NSA_HEADS).reshape(NSA_G, NSA_REP)


def _layer(xp, xs, c_all, sb_cache, nsa_cache, win_cache, page_table, w, nb, t, nbs, dec):
    d = xp.shape[1]
    n_p, n_s = xp.shape[0], xs.shape[0]
    slopes = _alibi_slopes()
    mod = _adaln(c_all, w["w_ada"], w["b_ada"])
    mod_p = mod[:nb].reshape(nb, 1, 6 * d)
    mod_s = jnp.repeat(mod[nb:nb + nbs], dec, axis=0)
    gains = {"q": w["nsa_q_g"], "ks": w["nsa_ks_g"], "kw": w["nsa_kw_g"]}
    w1k, w1v = _compress_weights(w["cmp_w1_k"]), _compress_weights(w["cmp_w1_v"])
    cpp = PAGE // CMP_STRIDE

    tm_p = _pick_tile(t, 256)
    h_p = _modulate(xp, w["norm1_g"], mod_p, False, nb, 1, 0, tm_p)
    q_sb, kv_sb_p, q_n, nsa_kv_p, win_kv_p, gates, gate_sb, gate_n = _project(h_p, w["w_in"], gains, d)
    o_sb = _sb_prompt(q_sb, kv_sb_p, nb, t)
    ident = jnp.arange(nb * (t // PAGE), dtype=I32).reshape(nb, t // PAGE)
    cmp_w = (*_pos_bias(w["cmp_pos_k"], w["cmp_pos_v"], w["cmp_w1_k"], w["cmp_w1_v"]), w["cmp_w2_k"], w["cmp_w2_v"],
             w["nsa_kc_g"])
    nsa_rows_p = nsa_kv_p.reshape(nb, t, 4, NSA_G, HEAD_DIM)
    hk, hv = _compress(nsa_rows_p.reshape(nb * t // PAGE, PAGE * 4 * NSA_G, HEAD_DIM), ident, w1k, w1v,
                       _pick_tile(t // PAGE, 8))
    nc_p = t // CMP_STRIDE
    kc, vc = _cmp_finish(hk, hv, (), 0, -(-nc_p // LANES) * LANES, *cmp_w)
    o_c, sel = _cmp_select(q_n, kc, vc, slopes, nb, t, 128, t // SEL_BLOCK, 0)
    o_n = _nsa_prompt(q_n, nsa_kv_p, win_kv_p, sel, o_c, gates, slopes, nb, t)
    mix_p = _merge_out(o_sb, o_n, gate_sb, gate_n, w)

    n_pages = page_table.shape[1]
    past = n_pages * PAGE
    h_s = _modulate(xs, w["norm1_g"], mod_s, True, 1, 1, 0, n_s)
    q_sb, kv_sb_s, q_n, nsa_kv_s, win_kv_s, gates, gate_sb, gate_n = _project(h_s, w["w_in"], gains, d)
    eye = jnp.eye(SB_HEADS, dtype=BF16)
    q_rhs = (q_sb.reshape(nbs, dec, SB_HEADS, HEAD_DIM).transpose(0, 2, 3, 1)[:, :, :, None, :]
             * eye[None, :, None, :, None]).reshape(nbs, SB_HEADS // 2, 2 * HEAD_DIM, SB_HEADS * dec)
    o_sb = _sb_sample(q_rhs, kv_sb_s, sb_cache.reshape(sb_cache.shape[0], -1, HEAD_DIM), page_table, dec)
    nsa_pages = nsa_cache.reshape(nsa_cache.shape[0], -1, HEAD_DIM)
    hk, hv = _compress(nsa_pages, page_table, w1k, w1v, _pick_tile(n_pages, 8))
    nsa_rows_s = nsa_kv_s.reshape(nbs, dec, 4, NSA_G, HEAD_DIM)
    tail = jnp.pad(nsa_rows_s.reshape(nbs, dec * 4 * NSA_G, HEAD_DIM), ((0, 0), (0, (PAGE - dec) * 4 * NSA_G), (0, 0)))
    n_tot = past + dec + (-(past + dec)) % SEL_BLOCK
    nc = n_tot // CMP_STRIDE
    nc_pad = -(-nc // LANES) * LANES
    n_tail = nc - n_pages * cpp
    assert 0 <= n_tail <= cpp
    tails = _compress(tail, jnp.arange(nbs, dtype=I32).reshape(nbs, 1), w1k, w1v, 1) if n_tail else ()
    kc, vc = _cmp_finish(hk, hv, tuple(tails), n_tail, nc_pad, *cmp_w)
    o_c, sel = _cmp_select(q_n, kc, vc, slopes, nbs, dec, dec, n_tot // SEL_BLOCK, past)
    o_n = _nsa_sample(q_n, sel, o_c, gates, nsa_kv_s, win_kv_s, win_cache.reshape(nbs, -1, HEAD_DIM),
                      nsa_pages, page_table, slopes, dec)
    mix_s = _merge_out(o_sb, o_n, gate_sb, gate_n, w)

    wr = jnp.pad(jnp.concatenate([w["w_route_group"], w["w_route_expert"]], axis=1),
                 ((0, 0), (0, LANES - MOE_GROUPS - N_EXPERTS)))
    br = jnp.pad(jnp.concatenate([w["b_route_group"], w["b_route_expert"]]),
                 (0, LANES - MOE_GROUPS - N_EXPERTS)).reshape(1, LANES)
    wr_hi, wr_lo = _split(wr)
    tm_r = _pick_tile(t, 128)
    x1_p, h2_p, rid_p, rw_p = _resid_route(xp, mix_p, w["norm2_g"], mod_p, False, nb, wr_hi, wr_lo, br, tm_r)
    x1_s, h2_s, rid_s, rw_s = _resid_route(xs, mix_s, w["norm2_g"], mod_s, True, 1, wr_hi, wr_lo, br,
                                           _pick_tile(n_s, 128))
    h2 = jnp.concatenate([h2_p, h2_s], axis=0)
    rid = jnp.concatenate([rid_p[:, :2], rid_s[:, :2]], axis=0)
    rw = jnp.concatenate([rw_p[:, :2], rw_s[:, :2]], axis=0)
    y = _moe(h2, rid, rw, w["w_exp_gate"], w["w_exp_up"], w["w_exp_down"])
    n_all = n_p + n_s
    tm_f = _pick_tile(n_s, _pick_tile(t, 256))
    y_p = _final(x1_p, mod_p, False, nb, y, 0, n_all, tm_f)
    y_s = _final(x1_s, mod_s, True, 1, y, n_p, n_all, tm_f)

    win_state_p = win_kv_p.reshape(nb, t, -1)[:, t - min(WINDOW, t):]
    win_state_s = jnp.concatenate([win_cache[:, dec:], win_kv_s.reshape((nbs, dec) + win_cache.shape[2:])], axis=1)
    state = (kv_sb_p, nsa_rows_p, win_state_p, kv_sb_s, nsa_rows_s, win_state_s)
    return y_p, y_s, state


def kernel(x_prompt, x_sample, c_prompt, c_sample, cache_sb_kv, cache_nsa_kv, cache_win_kv, page_table, norm1_g, norm2_g, w_ada, b_ada, w_in, nsa_q_g, nsa_kc_g, nsa_ks_g, nsa_kw_g, cmp_pos_k, cmp_pos_v, cmp_w1_k, cmp_w2_k, cmp_w1_v, cmp_w2_v, w_br_sb, w_br_nsa, w_out, w_route_group, b_route_group, w_route_expert, b_route_expert, w_exp_gate, w_exp_up, w_exp_down):
    nb, t, d = x_prompt.shape
    nbs, dec, _ = x_sample.shape
    depth = w_in.shape[0]
    weights = dict(norm1_g=norm1_g, norm2_g=norm2_g, w_ada=w_ada, b_ada=b_ada, w_in=w_in, nsa_q_g=nsa_q_g,
                   nsa_kc_g=nsa_kc_g, nsa_ks_g=nsa_ks_g, nsa_kw_g=nsa_kw_g, cmp_pos_k=cmp_pos_k,
                   cmp_pos_v=cmp_pos_v, cmp_w1_k=cmp_w1_k, cmp_w2_k=cmp_w2_k, cmp_w1_v=cmp_w1_v,
                   cmp_w2_v=cmp_w2_v, w_br_sb=w_br_sb, w_br_nsa=w_br_nsa, w_out=w_out,
                   w_route_group=w_route_group, b_route_group=b_route_group, w_route_expert=w_route_expert,
                   b_route_expert=b_route_expert, w_exp_gate=w_exp_gate, w_exp_up=w_exp_up, w_exp_down=w_exp_down)
    n_c = nb + nbs
    c_all = jnp.pad(jnp.concatenate([c_prompt, c_sample], axis=0), ((0, (-n_c) % 8), (0, 0)))
    xp = x_prompt.reshape(nb * t, d)
    xs = x_sample.reshape(nbs * dec, d)
    states = []
    for layer in range(depth):
        w = {k: v[layer] for k, v in weights.items()}
        xp, xs, st = _layer(xp, xs, c_all, cache_sb_kv[layer], cache_nsa_kv[layer], cache_win_kv[layer],
                            page_table, w, nb, t, nbs, dec)
        states.append(st)
    g = NSA_G
    stack = lambda k, shape: jnp.stack([s[k].reshape(shape) for s in states])
    return (xp.reshape(nb, t, d), xs.reshape(nbs, dec, d),
            stack(0, (nb, t, 2, SB_HEADS, HEAD_DIM)),
            stack(1, (nb, t, 4, g, HEAD_DIM)),
            stack(2, (nb, min(WINDOW, t), 2, g, HEAD_DIM)),
            stack(3, (nbs, dec, 2, SB_HEADS, HEAD_DIM)),
            stack(4, (nbs, dec, 4, g, HEAD_DIM)),
            stack(5, (nbs, cache_win_kv.shape[2], 2, g, HEAD_DIM)))
```

```python
import functools

import jax
import jax.numpy as jnp
from jax import lax
from jax.experimental import pallas as pl
from jax.experimental.pallas import tpu as pltpu

F32 = jnp.float32
BF16 = jnp.bfloat16
I32 = jnp.int32

HEAD_DIM = 128
SB_HEADS = 16
NSA_HEADS = 16
NSA_G = 2
NSA_REP = NSA_HEADS // NSA_G
CMP_STRIDE = 16
CMP_BLOCK = 2 * CMP_STRIDE
CMP_HIDDEN = 256
SEL_BLOCK = 64
SEL_RATIO = SEL_BLOCK // CMP_STRIDE
SEL_TOPK = 16
WINDOW = 512
PAGE = 128
MOE_GROUPS = 4
EPG = 8
N_EXPERTS = MOE_GROUPS * EPG
EXPERT_FF = 512
NORM_EPS = 1e-6
NEG_INF = -1e30
FORCE_SCORE = 1e4
LOWEST = -3e38
SCALE = HEAD_DIM ** -0.5
LANES = 128
SUBLANES = 8
VMEM_LIMIT = 48 * 1024 * 1024
MOE_VMEM_LIMIT = 56 * 1024 * 1024


def _cparams(*sem):
    return pltpu.CompilerParams(dimension_semantics=sem, vmem_limit_bytes=VMEM_LIMIT)


def _dot(a, b):
    return jnp.dot(a, b, preferred_element_type=F32)


def _dot_t(a, b):
    return lax.dot_general(a, b, (((1,), (1,)), ((), ())), preferred_element_type=F32)


def _split(x):
    hi = x.astype(BF16)
    lo = (x - hi.astype(F32)).astype(BF16)
    return hi, lo


def _dot3(a, b):
    a_hi, a_lo = _split(a)
    b_hi, b_lo = _split(b)
    return _dot(a_hi, b_hi) + _dot(a_lo, b_hi) + _dot(a_hi, b_lo)


def _dot2_exact_rhs(a, b_bf16):
    a_hi, a_lo = _split(a)
    return _dot(a_hi, b_bf16) + _dot(a_lo, b_bf16)


def _sigmoid(x):
    return 1.0 / (1.0 + jnp.exp(-x))


def _softplus(x):
    return jnp.maximum(x, 0.0) + jnp.log(1.0 + jnp.exp(-jnp.abs(x)))


def _rms(x, gain):
    return x * lax.rsqrt(jnp.mean(x * x, axis=-1, keepdims=True) + NORM_EPS) * gain


def _masked_softmax(s, mask):
    s = jnp.where(mask, s, NEG_INF)
    m = jnp.max(s, axis=-1, keepdims=True)
    e = jnp.where(mask, jnp.exp(s - m), 0.0)
    l = jnp.sum(e, axis=-1, keepdims=True)
    return e / jnp.where(l > 0.0, l, 1.0)


def _as2d(v):
    return v.reshape(v.shape[-2], v.shape[-1])


def _adaln_kernel(c_ref, w_ref, b_ref, o_ref):
    c = c_ref[...]
    o_ref[...] = _dot3(c * _sigmoid(c), w_ref[...]) + b_ref[...]


def _adaln(c_pad, w_ada, b_ada):
    m, d = c_pad.shape
    n = w_ada.shape[1]
    tn = 512
    return pl.pallas_call(
        _adaln_kernel,
        out_shape=jax.ShapeDtypeStruct((m, n), F32),
        grid=(n // tn,),
        in_specs=[pl.BlockSpec((m, d), lambda j: (0, 0)),
                  pl.BlockSpec((d, tn), lambda j: (0, j)),
                  pl.BlockSpec((1, tn), lambda j: (0, j))],
        out_specs=pl.BlockSpec((m, tn), lambda j: (0, j)),
        compiler_params=_cparams("arbitrary"),
        name="adaln",
    )(c_pad, w_ada, b_ada.reshape(1, n))


def _modulate_kernel(x_ref, g_ref, scale_ref, shift_ref, h_ref):
    h = _rms(x_ref[...], g_ref[...]) * (1.0 + _as2d(scale_ref[...])) + _as2d(shift_ref[...])
    h_ref[...] = h.astype(h_ref.dtype)


def _mod_spec(per_token, tm, d, col, nt):
    if per_token:
        return pl.BlockSpec((tm, d), lambda b, i: (b * nt + i, col))
    return pl.BlockSpec((1, 1, d), lambda b, i: (b, 0, col))


def _modulate(x, gain, mod, per_token, nb, scale_col, shift_col, tm):
    m, d = x.shape
    nt = m // nb // tm
    row = pl.BlockSpec((tm, d), lambda b, i: (b * nt + i, 0))
    return pl.pallas_call(
        _modulate_kernel,
        out_shape=jax.ShapeDtypeStruct((m, d), BF16),
        grid=(nb, nt),
        in_specs=[row, pl.BlockSpec((1, d), lambda b, i: (0, 0)),
                  _mod_spec(per_token, tm, d, scale_col, nt), _mod_spec(per_token, tm, d, shift_col, nt)],
        out_specs=row,
        compiler_params=_cparams("arbitrary", "arbitrary"),
        name="modulate",
    )(x, gain.reshape(1, d), mod, mod)


def _mm_kernel(*refs, n_pairs, n_extra, epilogue):
    xs = refs[:n_pairs]
    ws = refs[n_pairs:2 * n_pairs]
    extras = refs[2 * n_pairs:2 * n_pairs + n_extra]
    o_ref = refs[2 * n_pairs + n_extra]
    wbfs = refs[2 * n_pairs + n_extra + 1:]

    @pl.when(pl.program_id(1) == 0)
    def _():
        for w, wb in zip(ws, wbfs):
            wb[...] = w[...].astype(BF16)

    accs = [_dot(x[...], wb[...]) for x, wb in zip(xs, wbfs)]
    o_ref[...] = epilogue(accs, [e[...] for e in extras]).astype(o_ref.dtype)


def _matmul(xs, ws, col_off, n_cols, extras, epilogue, out_dtype, tm, tn, name):
    m = xs[0].shape[0]
    assert m % tm == 0 and n_cols % tn == 0 and col_off % tn == 0
    joff = col_off // tn
    in_specs = [pl.BlockSpec((tm, x.shape[1]), lambda j, i: (i, 0)) for x in xs]
    in_specs += [pl.BlockSpec((w.shape[0], tn), lambda j, i: (0, joff + j)) for w in ws]
    for kind, arr in extras:
        if kind == "col":
            in_specs.append(pl.BlockSpec((1, tn), lambda j, i: (0, j)))
        else:
            in_specs.append(pl.BlockSpec((tm, tn), lambda j, i: (i, j)))
    return pl.pallas_call(
        functools.partial(_mm_kernel, n_pairs=len(xs), n_extra=len(extras), epilogue=epilogue),
        out_shape=jax.ShapeDtypeStruct((m, n_cols), out_dtype),
        grid=(n_cols // tn, m // tm),
        in_specs=in_specs,
        out_specs=pl.BlockSpec((tm, tn), lambda j, i: (i, j)),
        scratch_shapes=[pltpu.VMEM((w.shape[0], tn), BF16) for w in ws],
        compiler_params=_cparams("arbitrary", "arbitrary"),
        name=name,
    )(*xs, *ws, *[a for _, a in extras])


def _ep_plain(accs, extras):
    return accs[0]


def _ep_scaled(accs, extras):
    return accs[0] * SCALE


def _ep_sigmoid(accs, extras):
    return _sigmoid(accs[0])


def _ep_gated_pair(accs, extras):
    return extras[0].astype(F32) * accs[0] + extras[1].astype(F32) * accs[1]


def _ep_groupnorm(accs, extras, post_scale):
    acc = accs[0]
    gain, flag = extras
    outs = []
    for c in range(acc.shape[1] // LANES):
        sl = slice(c * LANES, (c + 1) * LANES)
        x = acc[:, sl]
        outs.append(jnp.where(flag[:, sl] > 0.0, _rms(x, gain[:, sl]), x) * post_scale)
    return jnp.concatenate(outs, axis=1)


def _upper_ones(n):
    j = lax.broadcasted_iota(I32, (n, n), 0)
    s = lax.broadcasted_iota(I32, (n, n), 1)
    return jnp.where(j >= s, 1.0, 0.0).astype(BF16)


def _sb_block(q, k, v, ones_u, later, mask):
    z = _dot_t(q, k)
    sp = _softplus(z)
    if mask is not None:
        sp = jnp.where(mask, sp, 0.0)
    csum = _dot(sp.astype(BF16), ones_u)
    a = jnp.exp(z - csum - later)
    if mask is not None:
        a = jnp.where(mask, a, 0.0)
    return _dot(a.astype(BF16), v), later + csum[:, 0:1]


SB_HEADS_PER_STEP = 4


def _sb_prompt_kernel(q_ref, k_ref, v_ref, o_ref, *, tq):
    qi = pl.program_id(2)
    ones_u = _upper_ones(tq)
    row = lax.broadcasted_iota(I32, (tq, tq), 0)
    col = lax.broadcasted_iota(I32, (tq, tq), 1)
    heads = [slice(h * HEAD_DIM, (h + 1) * HEAD_DIM) for h in range(SB_HEADS_PER_STEP)]
    qs = [q_ref[:, sl] for sl in heads]

    def kv(kj, sl):
        start = pl.multiple_of(kj * tq, tq)
        return k_ref[pl.ds(start, tq), sl].astype(BF16), v_ref[pl.ds(start, tq), sl].astype(BF16)

    carry = []
    for q, sl in zip(qs, heads):
        k, v = kv(qi, sl)
        carry.extend(_sb_block(q, k, v, ones_u, jnp.zeros((tq, 1), F32), col < row))

    def body(step, carry):
        new = []
        for n, (q, sl) in enumerate(zip(qs, heads)):
            k, v = kv(qi - 1 - step, sl)
            out, later = _sb_block(q, k, v, ones_u, carry[2 * n + 1], None)
            new.extend((carry[2 * n] + out, later))
        return tuple(new)

    carry = lax.fori_loop(0, qi, body, tuple(carry))
    for n, sl in enumerate(heads):
        o_ref[:, sl] = carry[2 * n].astype(o_ref.dtype)


def _sb_prompt(q_sb, kv_sb, nb, t):
    tq = 256
    nq = t // tq
    width = SB_HEADS_PER_STEP * HEAD_DIM
    n_hp = SB_HEADS // SB_HEADS_PER_STEP
    return pl.pallas_call(
        functools.partial(_sb_prompt_kernel, tq=tq),
        out_shape=jax.ShapeDtypeStruct(q_sb.shape, BF16),
        grid=(nb, n_hp, nq),
        in_specs=[pl.BlockSpec((tq, width), lambda b, h, i: (b * nq + i, h)),
                  pl.BlockSpec((t, width), lambda b, h, i: (b, h)),
                  pl.BlockSpec((t, width), lambda b, h, i: (b, n_hp + h))],
        out_specs=pl.BlockSpec((tq, width), lambda b, h, i: (b * nq + i, h)),
        compiler_params=_cparams("arbitrary", "arbitrary", "arbitrary"),
        name="sb_prompt",
    )(q_sb, kv_sb, kv_sb)


def _later_ones(n):
    s = lax.broadcasted_iota(I32, (n, n), 0)
    j = lax.broadcasted_iota(I32, (n, n), 1)
    return jnp.where(j >= s, 1.0, 0.0).astype(BF16)


def _sb_sample_kernel(pt_ref, qr_ref, new_ref, *refs, n_pg, dec):
    pages = refs[:n_pg]
    o_ref = refs[n_pg]
    acc_ref, later_ref, rows_ref = refs[n_pg + 1:]
    j = pl.program_id(1)
    cols = SB_HEADS * dec
    rows_per_tok = 2 * SB_HEADS
    n_tok = n_pg * PAGE
    pitch = n_tok + SUBLANES

    def process(k_heads, v_heads, n_keys, mask):
        pair = lambda xs, h: jnp.concatenate([xs[h], xs[h + 1]], axis=1)
        z = _dot(pair(k_heads, 0), qr_ref[0, 0])
        for h in range(2, SB_HEADS, 2):
            z = z + _dot(pair(k_heads, h), qr_ref[0, h // 2])
        sp = _softplus(z)
        if mask is not None:
            sp = jnp.where(mask, sp, 0.0)
        sp_hi, sp_lo = _split(sp)
        ones_l = _later_ones(n_keys)
        csum = _dot(ones_l, sp_hi) + _dot(ones_l, sp_lo)
        a = jnp.exp(z - csum - later_ref[...])
        if mask is not None:
            a = jnp.where(mask, a, 0.0)
        a = a.T.astype(BF16)
        for h in range(0, SB_HEADS, 2):
            both = _dot(a[h * dec:(h + 2) * dec], pair(v_heads, h))
            acc_ref[h * dec:(h + 1) * dec, :] += both[:dec, :HEAD_DIM]
            acc_ref[(h + 1) * dec:(h + 2) * dec, :] += both[dec:, HEAD_DIM:]
        later_ref[...] += csum[0:1, :]

    @pl.when(j == 0)
    def _():
        acc_ref[...] = jnp.zeros_like(acc_ref)
        later_ref[...] = jnp.zeros_like(later_ref)
        pad = jnp.zeros((PAGE - dec, HEAD_DIM), F32)
        new = new_ref[...]
        head = lambda c: jnp.concatenate([new[:, c * HEAD_DIM:(c + 1) * HEAD_DIM], pad], axis=0).astype(BF16)
        s = lax.broadcasted_iota(I32, (PAGE, cols), 0)
        t = lax.broadcasted_iota(I32, (PAGE, cols), 1) % dec
        process([head(h) for h in range(SB_HEADS)], [head(SB_HEADS + h) for h in range(SB_HEADS)], PAGE, s < t)

    for p in range(n_pg):
        page = pages[n_pg - 1 - p].at[0]
        for tok in range(PAGE):
            for c0 in range(0, rows_per_tok, SUBLANES):
                rows_ref[pl.ds(c0 * pitch + p * PAGE + tok, SUBLANES, stride=pitch), :] = (
                    page[pl.ds(tok * rows_per_tok + c0, SUBLANES), :])

    def rows_of(c):
        return rows_ref[pl.ds(c * pitch, n_tok), :].astype(BF16)

    process([rows_of(h) for h in range(SB_HEADS)], [rows_of(SB_HEADS + h) for h in range(SB_HEADS)],
            n_tok, None)

    @pl.when(j == pl.num_programs(1) - 1)
    def _():
        for h in range(SB_HEADS):
            o_ref[:, h * HEAD_DIM:(h + 1) * HEAD_DIM] = acc_ref[h * dec:(h + 1) * dec, :].astype(o_ref.dtype)


def _sb_sample(q_rhs, kv_new, cache, page_table, dec):
    nb, n_pages = page_table.shape
    n_pg = _pick_tile(n_pages, 4)
    width = SB_HEADS * HEAD_DIM
    cols = SB_HEADS * dec
    assert cols == LANES
    page_rows = cache.shape[1]

    def page_spec(p):
        return pl.BlockSpec((1, page_rows, HEAD_DIM),
                            lambda b, j, pt: (pt[b, n_pages - 1 - (j * n_pg + p)], 0, 0))

    grid_spec = pltpu.PrefetchScalarGridSpec(
        num_scalar_prefetch=1,
        grid=(nb, n_pages // n_pg),
        in_specs=[pl.BlockSpec((1, SB_HEADS // 2, 2 * HEAD_DIM, cols), lambda b, j, pt: (b, 0, 0, 0)),
                  pl.BlockSpec((dec, 2 * width), lambda b, j, pt: (b, 0))]
        + [page_spec(p) for p in range(n_pg)],
        out_specs=pl.BlockSpec((dec, width), lambda b, j, pt: (b, 0)),
        scratch_shapes=[pltpu.VMEM((cols, HEAD_DIM), F32), pltpu.VMEM((1, cols), F32),
                        pltpu.VMEM((2 * SB_HEADS * (n_pg * PAGE + SUBLANES), HEAD_DIM), F32)],
    )
    return pl.pallas_call(
        functools.partial(_sb_sample_kernel, n_pg=n_pg, dec=dec),
        out_shape=jax.ShapeDtypeStruct((nb * dec, width), BF16),
        grid_spec=grid_spec,
        compiler_params=_cparams("arbitrary", "arbitrary"),
        name="sb_sample",
    )(page_table, q_rhs, kv_new, *([cache] * n_pg))


def _compress_kernel(pt_ref, *refs, n_pg):
    pages = refs[:n_pg]
    w_refs = refs[n_pg:n_pg + 2]
    out_refs = refs[n_pg + 2:n_pg + 4]
    cpp = PAGE // CMP_STRIDE
    half = n_pg * cpp
    rows_per_tok = 4 * NSA_G
    for slot in range(2):
        acc = jnp.zeros((2 * half, 2 * CMP_HIDDEN), F32)
        for r in range(CMP_STRIDE):
            pieces = []
            for g in range(NSA_G):
                first = r * rows_per_tok + slot * NSA_G + g
                for p in range(n_pg):
                    pieces.append(pages[p].at[0][pl.ds(first, cpp, stride=CMP_STRIDE * rows_per_tok), :])
            x = jnp.concatenate(pieces, axis=0).astype(BF16)
            acc = acc + _dot(x, w_refs[slot][r])
        for g in range(NSA_G):
            out_refs[slot][0, g] = acc[g * half:(g + 1) * half]


def _compress(rows, table, w_k, w_v, n_pg):
    nb, n_pages = table.shape
    assert n_pages % n_pg == 0
    cpp = PAGE // CMP_STRIDE
    page_rows = rows.shape[1]

    def page_spec(p):
        return pl.BlockSpec((1, page_rows, HEAD_DIM), lambda b, j, pt: (pt[b, j * n_pg + p], 0, 0))

    w_spec = pl.BlockSpec((CMP_STRIDE, HEAD_DIM, 2 * CMP_HIDDEN), lambda b, j, pt: (0, 0, 0))
    out_spec = pl.BlockSpec((1, NSA_G, n_pg * cpp, 2 * CMP_HIDDEN), lambda b, j, pt: (b, 0, j, 0))
    out_sds = jax.ShapeDtypeStruct((nb, NSA_G, n_pages * cpp, 2 * CMP_HIDDEN), F32)
    grid_spec = pltpu.PrefetchScalarGridSpec(
        num_scalar_prefetch=1,
        grid=(nb, n_pages // n_pg),
        in_specs=[page_spec(p) for p in range(n_pg)] + [w_spec, w_spec],
        out_specs=[out_spec, out_spec],
    )
    return pl.pallas_call(
        functools.partial(_compress_kernel, n_pg=n_pg),
        out_shape=[out_sds, out_sds],
        grid_spec=grid_spec,
        compiler_params=_cparams("arbitrary", "arbitrary"),
        name="nsa_compress",
    )(table, *([rows] * n_pg), w_k, w_v)


def _gelu_tanh(x):
    return 0.5 * x * (1.0 + jnp.tanh(0.7978845608028654 * (x + 0.044715 * x * x * x)))


def _cmp_finish_kernel(*refs, nc_pad, n_tail):
    if n_tail:
        hk_ref, hv_ref, tk_ref, tv_ref = refs[:4]
        refs = refs[4:]
    else:
        hk_ref, hv_ref = refs[:2]
        tk_ref = tv_ref = None
        refs = refs[2:]
    bk_ref, bv_ref, w2k_ref, w2v_ref, g_ref, kc_ref, vc_ref = refs
    row = lax.broadcasted_iota(I32, (nc_pad, CMP_HIDDEN), 0)

    def chunks(h_ref, t_ref):
        parts = [h_ref[0, 0]]
        if t_ref is not None:
            tail = t_ref[0, 0]
            keep = lax.broadcasted_iota(I32, tail.shape, 0) < n_tail
            parts.append(jnp.where(keep, tail, 0.0))
        n = sum(p.shape[0] for p in parts)
        if nc_pad > n:
            parts.append(jnp.zeros((nc_pad - n, 2 * CMP_HIDDEN), F32))
        return jnp.concatenate(parts, axis=0) if len(parts) > 1 else parts[0]

    def branch(h_ref, t_ref, b_ref, w2_ref):
        h = chunks(h_ref, t_ref)
        nxt = pltpu.roll(h[:, CMP_HIDDEN:], nc_pad - 1, 0)
        nxt = jnp.where(row == nc_pad - 1, 0.0, nxt)
        hid = _gelu_tanh(h[:, :CMP_HIDDEN] + nxt + b_ref[0:1, :])
        return _dot(hid.astype(BF16), w2_ref[...].astype(BF16))

    kc_ref[0, 0] = _rms(branch(hk_ref, tk_ref, bk_ref, w2k_ref), g_ref[...])
    vc_ref[0, 0] = branch(hv_ref, tv_ref, bv_ref, w2v_ref)


def _pos_bias_kernel(posk_ref, posv_ref, w1k_ref, w1v_ref, bk_ref, bv_ref):
    bk_ref[...] = _dot3(posk_ref[...], w1k_ref[...])
    bv_ref[...] = _dot3(posv_ref[...], w1v_ref[...])


def _pos_bias(pos_k, pos_v, w1k, w1v):
    pos_k8 = jnp.tile(pos_k.reshape(1, -1), (SUBLANES, 1))
    pos_v8 = jnp.tile(pos_v.reshape(1, -1), (SUBLANES, 1))
    sds = jax.ShapeDtypeStruct((SUBLANES, CMP_HIDDEN), F32)
    return pl.pallas_call(_pos_bias_kernel, out_shape=[sds, sds],
                          compiler_params=pltpu.CompilerParams(vmem_limit_bytes=VMEM_LIMIT),
                          name="nsa_pos_bias")(pos_k8, pos_v8, w1k, w1v)


def _cmp_finish(hk, hv, tails, n_tail, nc_pad, bias_k, bias_v, w2k, w2v, kc_g):
    nb, _, n_chunks, _ = hk.shape
    assert n_chunks % SUBLANES == 0 and nc_pad % SUBLANES == 0
    h_spec = pl.BlockSpec((1, 1, n_chunks, 2 * CMP_HIDDEN), lambda b, g: (b, g, 0, 0))
    t_specs = [pl.BlockSpec((1, 1) + t.shape[2:], lambda b, g: (b, g, 0, 0)) for t in tails]
    o_spec = pl.BlockSpec((1, 1, nc_pad, HEAD_DIM), lambda b, g: (b, g, 0, 0))
    o_sds = jax.ShapeDtypeStruct((nb, NSA_G, nc_pad, HEAD_DIM), F32)

    def full(a):
        return pl.BlockSpec(a.shape, lambda b, g: (0,) * a.ndim)

    kc_g = kc_g.reshape(1, HEAD_DIM)
    args = (bias_k, bias_v, w2k, w2v, kc_g)
    return pl.pallas_call(
        functools.partial(_cmp_finish_kernel, nc_pad=nc_pad, n_tail=n_tail if tails else 0),
        out_shape=[o_sds, o_sds],
        grid=(nb, NSA_G),
        in_specs=[h_spec, h_spec] + t_specs + [full(a) for a in args],
        out_specs=[o_spec, o_spec],
        compiler_params=_cparams("arbitrary", "arbitrary"),
        name="nsa_cmp_finish",
    )(hk, hv, *tails, *args)


def _cmp_select_kernel(slopes_ref, q_ref, kc_ref, vc_ref, oc_ref, sel_ref, *, tr, nc_pad, ns_pad, ns, base):
    g = pl.program_id(1)
    i = pl.program_id(2)
    q = q_ref[...]
    kc = kc_ref[0, 0].astype(BF16)
    vc = vc_ref[0, 0].astype(BF16)
    pos = base + i * tr + lax.broadcasted_iota(I32, (tr, 1), 0)
    end = lax.broadcasted_iota(I32, (1, nc_pad), 1) * CMP_STRIDE + (CMP_BLOCK - 1)
    dist = (pos - end).astype(F32)
    valid = dist >= 0.0
    score = jnp.zeros((tr, nc_pad), F32)
    for r in range(NSA_REP):
        sl = slice(r * HEAD_DIM, (r + 1) * HEAD_DIM)
        s = _dot_t(q[:, sl], kc) - slopes_ref[g, r] * dist
        p = _masked_softmax(s, valid)
        oc_ref[:, sl] = _dot(p.astype(BF16), vc)
        score = score + p
    n_id = lax.broadcasted_iota(I32, (nc_pad, ns_pad), 0)
    j_id = lax.broadcasted_iota(I32, (nc_pad, ns_pad), 1)
    group = jnp.where((n_id // SEL_RATIO) == j_id, 1.0, 0.0).astype(BF16)
    sc = _dot2_exact_rhs(score, group)
    j = lax.broadcasted_iota(I32, (tr, ns_pad), 1)
    cur = pos // SEL_BLOCK
    forced = (j == 0) | (j == cur) | (j == cur - 1)
    sc = jnp.where(j > cur, NEG_INF, jnp.where(forced, FORCE_SCORE, sc))
    sc = jnp.where(j < ns, sc, LOWEST)
    rank = jnp.zeros((tr, ns_pad), F32)
    for c in range(ns):
        other = sc[:, c:c + 1]
        ahead = (other > sc) | ((other == sc) & (c < j))
        rank = rank + jnp.where(ahead, 1.0, 0.0)
    sel_ref[0, 0] = jnp.where((rank < float(min(SEL_TOPK, ns))) & (j < ns), 1.0, 0.0)


def _cmp_select(q_n, kc, vc, slopes, nb, t, tr, ns, base):
    nc_pad = kc.shape[2]
    ns_pad = -(-ns // LANES) * LANES
    nt = t // tr
    width = NSA_REP * HEAD_DIM
    kv_spec = pl.BlockSpec((1, 1, nc_pad, HEAD_DIM), lambda b, g, i: (b, g, 0, 0))
    return pl.pallas_call(
        functools.partial(_cmp_select_kernel, tr=tr, nc_pad=nc_pad, ns_pad=ns_pad, ns=ns, base=base),
        out_shape=[jax.ShapeDtypeStruct(q_n.shape, F32),
                   jax.ShapeDtypeStruct((nb, NSA_G, t, ns_pad), F32)],
        grid=(nb, NSA_G, nt),
        in_specs=[pl.BlockSpec(memory_space=pltpu.SMEM),
                  pl.BlockSpec((tr, width), lambda b, g, i: (b * nt + i, g)), kv_spec, kv_spec],
        out_specs=[pl.BlockSpec((tr, width), lambda b, g, i: (b * nt + i, g)),
                   pl.BlockSpec((1, 1, tr, ns_pad), lambda b, g, i: (b, g, i, 0))],
        compiler_params=_cparams("arbitrary", "arbitrary", "arbitrary"),
        name="nsa_cmp_select",
    )(slopes, q_n, kc, vc)


def _block_expand(n_blocks, key_pos):
    j = lax.broadcasted_iota(I32, (n_blocks, key_pos.shape[1]), 0)
    return jnp.where(j == key_pos // SEL_BLOCK, 1.0, 0.0).astype(BF16)


def _biased_attention(qh, k, v, slope, neg_dist):
    s = _dot_t(qh, k) + slope * neg_dist
    e = jnp.exp(s - jnp.max(s, axis=-1, keepdims=True))
    return _dot(e.astype(BF16), v) / jnp.sum(e, axis=-1, keepdims=True)


def _nsa_prompt_kernel(slopes_ref, q_ref, sk_ref, sv_ref, wk_ref, wv_ref, sel_ref, oc_ref, gate_ref, o_ref,
                       part_ref, *, tq, t, n_cls):
    g = pl.program_id(1)
    i = pl.program_id(2)
    t0 = i * tq
    span = WINDOW + tq
    qpos = t0 + lax.broadcasted_iota(I32, (tq, 1), 0)
    gates = gate_ref[0]
    heads = [slice(r * HEAD_DIM, (r + 1) * HEAD_DIM) for r in range(NSA_REP)]

    wstart = pl.multiple_of(jnp.clip(t0 - WINDOW, 0, t - span), tq)
    wk = wk_ref[pl.ds(wstart, span), :].astype(BF16)
    wv = wv_ref[pl.ds(wstart, span), :].astype(BF16)
    dist_w = (qpos - (wstart + lax.broadcasted_iota(I32, (1, span), 1))).astype(F32)
    nd_w = jnp.where((dist_w >= 0.0) & (dist_w < float(WINDOW)), -dist_w, NEG_INF)
    for r, sl in enumerate(heads):
        o_w = _biased_attention(q_ref[:, sl], wk, wv, slopes_ref[g, r], nd_w)
        part_ref[:, sl] = gates[:, 3 * r:3 * r + 1] * oc_ref[:, sl] + gates[:, 3 * r + 2:3 * r + 3] * o_w

    def selected(n_keys):
        sk = sk_ref[pl.ds(0, n_keys), :].astype(BF16)
        sv = sv_ref[pl.ds(0, n_keys), :].astype(BF16)
        kpos = lax.broadcasted_iota(I32, (1, n_keys), 1)
        dist = (qpos - kpos).astype(F32)
        sel = sel_ref[0, 0].astype(BF16)
        picked = _dot(sel, _block_expand(sel.shape[1], kpos))
        nd = jnp.where((dist >= 0.0) & (picked > 0.5), -dist, NEG_INF)
        for r, sl in enumerate(heads):
            o_s = _biased_attention(q_ref[:, sl], sk, sv, slopes_ref[g, r], nd)
            o_ref[:, sl] = (part_ref[:, sl] + gates[:, 3 * r + 1:3 * r + 2] * o_s).astype(o_ref.dtype)

    chunk = t // n_cls
    cls = ((i + 1) * tq - 1) // chunk
    for c in range(n_cls):
        pl.when(cls == c)(functools.partial(selected, (c + 1) * chunk))


def _nsa_prompt(q_n, nsa_kv, win_kv, sel, o_c, gates, slopes, nb, t):
    tq = 128
    assert t >= WINDOW + tq and t % tq == 0
    n_cls = 4 if t % (4 * tq) == 0 else 1
    nt = t // tq
    width = NSA_REP * HEAD_DIM
    ns_pad = sel.shape[-1]
    q_spec = pl.BlockSpec((tq, width), lambda b, g, i: (b * nt + i, g))

    def kv_spec(slot):
        return pl.BlockSpec((t, HEAD_DIM), lambda b, g, i: (b, slot * NSA_G + g))

    return pl.pallas_call(
        functools.partial(_nsa_prompt_kernel, tq=tq, t=t, n_cls=n_cls),
        out_shape=jax.ShapeDtypeStruct(q_n.shape, BF16),
        scratch_shapes=[pltpu.VMEM((tq, width), F32)],
        grid=(nb, NSA_G, nt),
        in_specs=[pl.BlockSpec(memory_space=pltpu.SMEM), q_spec,
                  kv_spec(2), kv_spec(3), kv_spec(0), kv_spec(1),
                  pl.BlockSpec((1, 1, tq, ns_pad), lambda b, g, i: (b, g, i, 0)),
                  q_spec,
                  pl.BlockSpec((1, tq, LANES), lambda b, g, i: (g, b * nt + i, 0))],
        out_specs=q_spec,
        compiler_params=_cparams("arbitrary", "arbitrary", "arbitrary"),
        name="nsa_prompt",
    )(slopes, q_n, nsa_kv, nsa_kv, win_kv, win_kv, sel, o_c, gates)


def _nsa_sample_kernel(pt_ref, slopes_ref, q_ref, sel_ref, oc_ref, gate_ref, new_ref, wnew_ref, wcache_ref,
                       *refs, n_pg, dec, past):
    pages = refs[:n_pg]
    o_ref = refs[n_pg]
    m_ref, l_ref, acc_ref = refs[n_pg + 1:]
    j = pl.program_id(1)
    rows = NSA_REP * dec
    ns_pad = sel_ref.shape[-1]
    row = lax.broadcasted_iota(I32, (rows, 1), 0)
    qpos = past + row % dec

    @pl.when(j == 0)
    def _():
        m_ref[...] = jnp.full_like(m_ref, NEG_INF)
        l_ref[...] = jnp.zeros_like(l_ref)
        acc_ref[...] = jnp.zeros_like(acc_ref)

    def stack_heads(x, g):
        return jnp.concatenate(
            [x[:, (g * NSA_REP + r) * HEAD_DIM:(g * NSA_REP + r + 1) * HEAD_DIM] for r in range(NSA_REP)], axis=0)

    def slope_col(g):
        col = jnp.zeros((rows, 1), F32)
        for r in range(NSA_REP):
            col = jnp.where(row // dec == r, slopes_ref[g, r], col)
        return col

    def online(g, s, mask, v):
        s = jnp.where(mask, s, NEG_INF)
        m_old = m_ref[g]
        m_new = jnp.maximum(m_old, jnp.max(s, axis=-1, keepdims=True))
        alpha = jnp.exp(m_old - m_new)
        e = jnp.where(mask, jnp.exp(s - m_new), 0.0)
        l_ref[g] = alpha * l_ref[g] + jnp.sum(e, axis=-1, keepdims=True)
        acc_ref[g] = alpha * acc_ref[g] + _dot(e.astype(BF16), v)
        m_ref[g] = m_new

    def sel_rows(g):
        return jnp.concatenate([sel_ref[0, g]] * NSA_REP, axis=0).astype(BF16)

    def attend_selected(g, qg, k, v, kpos, expand):
        dist = (qpos - kpos).astype(F32)
        picked = _dot(sel_rows(g), expand)
        s = _dot_t(qg, k) - slope_col(g) * dist
        online(g, s, (dist >= 0.0) & (picked > 0.5), v)

    q = q_ref[...]
    n_keys = n_pg * PAGE
    kpos_pages = j * n_keys + lax.broadcasted_iota(I32, (1, n_keys), 1)
    rows_per_tok = 4 * NSA_G

    def page_rows(c):
        return jnp.concatenate([pages[p].at[0][pl.ds(c, PAGE, stride=rows_per_tok), :] for p in range(n_pg)],
                               axis=0).astype(BF16)

    expand_pages = _block_expand(ns_pad, kpos_pages)
    for g in range(NSA_G):
        attend_selected(g, stack_heads(q, g), page_rows(2 * NSA_G + g), page_rows(3 * NSA_G + g), kpos_pages,
                        expand_pages)

    @pl.when(j == pl.num_programs(1) - 1)
    def _():
        pad = jnp.zeros((PAGE - dec, HEAD_DIM), F32)
        kpos_new = past + lax.broadcasted_iota(I32, (1, PAGE), 1)
        win_rows = 2 * NSA_G
        nbuf = wcache_ref.shape[1] // win_rows
        kpos_w = jnp.concatenate([past - nbuf + lax.broadcasted_iota(I32, (1, nbuf), 1), kpos_new], axis=1)
        dist_w = (qpos - kpos_w).astype(F32)
        key_ok = jnp.concatenate([jnp.ones((1, nbuf), F32),
                                  jnp.where(lax.broadcasted_iota(I32, (1, PAGE), 1) < dec, 1.0, 0.0)], axis=1)
        mask_w = (dist_w >= 0.0) & (dist_w < float(WINDOW)) & (kpos_w >= 0) & (key_ok > 0.5)
        expand_new = _block_expand(ns_pad, kpos_new)
        for g in range(NSA_G):
            qg = stack_heads(q, g)
            lane_k = (2 * NSA_G + g) * HEAD_DIM
            lane_v = (3 * NSA_G + g) * HEAD_DIM
            k_new = jnp.concatenate([new_ref[:, lane_k:lane_k + HEAD_DIM], pad], axis=0).astype(BF16)
            v_new = jnp.concatenate([new_ref[:, lane_v:lane_v + HEAD_DIM], pad], axis=0).astype(BF16)
            attend_selected(g, qg, k_new, v_new, kpos_new, expand_new)
            o_s = acc_ref[g] / l_ref[g]
            lw_k = g * HEAD_DIM
            lw_v = (NSA_G + g) * HEAD_DIM
            wk = jnp.concatenate([wcache_ref.at[0][pl.ds(g, nbuf, stride=win_rows), :],
                                  wnew_ref[:, lw_k:lw_k + HEAD_DIM], pad], axis=0).astype(BF16)
            wv = jnp.concatenate([wcache_ref.at[0][pl.ds(NSA_G + g, nbuf, stride=win_rows), :],
                                  wnew_ref[:, lw_v:lw_v + HEAD_DIM], pad], axis=0).astype(BF16)
            p_w = _masked_softmax(_dot_t(qg, wk) - slope_col(g) * dist_w, mask_w)
            o_w = _dot(p_w.astype(BF16), wv)
            gates = gate_ref[g]
            for r in range(NSA_REP):
                rs = slice(r * dec, (r + 1) * dec)
                sl = slice((g * NSA_REP + r) * HEAD_DIM, (g * NSA_REP + r + 1) * HEAD_DIM)
                out = (gates[:, 3 * r:3 * r + 1] * oc_ref[:, sl] + gates[:, 3 * r + 1:3 * r + 2] * o_s[rs]
                       + gates[:, 3 * r + 2:3 * r + 3] * o_w[rs])
                o_ref[:, sl] = out.astype(o_ref.dtype)


def _nsa_sample(q_n, sel, o_c, gates, nsa_new, win_new, win_cache, cache, page_table, slopes, dec):
    nb, n_pages = page_table.shape
    n_pg = _pick_tile(n_pages, 8)
    past = n_pages * PAGE
    rows = NSA_REP * dec
    ns_pad = sel.shape[-1]
    lanes = nsa_new.shape[-1]
    wl = win_new.shape[-1]
    page_rows = cache.shape[1]

    def page_spec(p):
        return pl.BlockSpec((1, page_rows, HEAD_DIM), lambda b, j, pt: (pt[b, j * n_pg + p], 0, 0))

    tok = lambda w: pl.BlockSpec((dec, w), lambda b, j, pt: (b, 0))
    grid_spec = pltpu.PrefetchScalarGridSpec(
        num_scalar_prefetch=1,
        grid=(nb, n_pages // n_pg),
        in_specs=[pl.BlockSpec(memory_space=pltpu.SMEM),
                  tok(NSA_HEADS * HEAD_DIM),
                  pl.BlockSpec((1, NSA_G, dec, ns_pad), lambda b, j, pt: (b, 0, 0, 0)),
                  tok(NSA_HEADS * HEAD_DIM),
                  pl.BlockSpec((NSA_G, dec, LANES), lambda b, j, pt: (0, b, 0)),
                  tok(lanes), tok(wl),
                  pl.BlockSpec((1, win_cache.shape[1], HEAD_DIM), lambda b, j, pt: (b, 0, 0))]
        + [page_spec(p) for p in range(n_pg)],
        out_specs=tok(NSA_HEADS * HEAD_DIM),
        scratch_shapes=[pltpu.VMEM((NSA_G, rows, 1), F32), pltpu.VMEM((NSA_G, rows, 1), F32),
                        pltpu.VMEM((NSA_G, rows, HEAD_DIM), F32)],
    )
    return pl.pallas_call(
        functools.partial(_nsa_sample_kernel, n_pg=n_pg, dec=dec, past=past),
        out_shape=jax.ShapeDtypeStruct(q_n.shape, BF16),
        grid_spec=grid_spec,
        compiler_params=_cparams("arbitrary", "arbitrary"),
        name="nsa_sample",
    )(page_table, slopes, q_n, sel, o_c, gates, nsa_new, win_new, win_cache, *([cache] * n_pg))


def _resid_route_kernel(x_ref, mix_ref, gate_ref, g_ref, scale_ref, shift_ref, wr_hi_ref, wr_lo_ref, br_ref,
                        x1_ref, h_ref, rid_ref, rw_ref):
    x1 = x_ref[...] + _as2d(gate_ref[...]) * mix_ref[...]
    x1_ref[...] = x1
    h = _rms(x1, g_ref[...]) * (1.0 + _as2d(scale_ref[...])) + _as2d(shift_ref[...])
    h_ref[...] = h
    h_hi, h_lo = _split(h)
    logits = _dot(h_hi, wr_hi_ref[...]) + _dot(h_lo, wr_hi_ref[...]) + _dot(h_hi, wr_lo_ref[...]) + br_ref[...]
    tm = logits.shape[0]
    lane = lax.broadcasted_iota(I32, (tm, LANES), 1).astype(F32)
    far = float(LANES)

    def first_lane(cond):
        return jnp.min(jnp.where(cond, lane, far), axis=-1, keepdims=True)

    in_groups = lane < float(MOE_GROUPS)
    lg = jnp.where(in_groups, logits, LOWEST)
    g_max = jnp.max(lg, axis=-1, keepdims=True)
    g_sum = jnp.sum(jnp.where(in_groups, jnp.exp(lg - g_max), 0.0), axis=-1, keepdims=True)
    g_top = 1.0 / g_sum
    g_idx = first_lane(in_groups & (lg == g_max))
    lo = float(MOE_GROUPS) + float(EPG) * g_idx
    in_experts = (lane >= lo) & (lane < lo + float(EPG))
    le = jnp.where(in_experts, logits, LOWEST)
    e_max = jnp.max(le, axis=-1, keepdims=True)
    e_sum = jnp.sum(jnp.where(in_experts, jnp.exp(le - e_max), 0.0), axis=-1, keepdims=True)
    i1 = first_lane(in_experts & (le == e_max))
    le2 = jnp.where(lane == i1, LOWEST, le)
    e_max2 = jnp.max(le2, axis=-1, keepdims=True)
    i2 = first_lane(in_experts & (lane != i1) & (le2 == e_max2))
    p1 = 1.0 / e_sum
    p2 = jnp.exp(e_max2 - e_max) / e_sum
    w1 = p1 / (p1 + p2) * g_top
    w2 = p2 / (p1 + p2) * g_top
    first = lane == 0.0
    second = lane == 1.0
    ids = jnp.where(first, i1 - float(MOE_GROUPS), jnp.where(second, i2 - float(MOE_GROUPS), 0.0))
    rid_ref[...] = ids.astype(I32)
    rw_ref[...] = jnp.where(first, w1, jnp.where(second, w2, 0.0))


def _resid_route(x, mix, gain, mod, per_token, nb, wr_hi, wr_lo, br, tm):
    m, d = x.shape
    nt = m // nb // tm
    row = pl.BlockSpec((tm, d), lambda b, i: (b * nt + i, 0))
    lane_row = pl.BlockSpec((tm, LANES), lambda b, i: (b * nt + i, 0))
    const = lambda a: pl.BlockSpec(a.shape, lambda b, i: (0, 0))
    gain = gain.reshape(1, d)
    return pl.pallas_call(
        _resid_route_kernel,
        out_shape=[jax.ShapeDtypeStruct((m, d), F32), jax.ShapeDtypeStruct((m, d), F32),
                   jax.ShapeDtypeStruct((m, LANES), I32), jax.ShapeDtypeStruct((m, LANES), F32)],
        grid=(nb, nt),
        in_specs=[row, row, _mod_spec(per_token, tm, d, 2, nt), const(gain),
                  _mod_spec(per_token, tm, d, 4, nt), _mod_spec(per_token, tm, d, 3, nt),
                  const(wr_hi), const(wr_lo), const(br)],
        out_specs=[row, row, lane_row, lane_row],
        compiler_params=_cparams("arbitrary", "arbitrary"),
        name="resid_route",
    )(x, mix, mod, gain, mod, mod, wr_hi, wr_lo, br)


def _moe_kernel(texp_ref, nused_ref, first_ref, src_ref, dst_ref, h_hbm, roww_ref, wg_ref, wu_ref, wd_ref, y_hbm,
                xbuf, xbf, wg_bf, wu_bf, wd_bf, sem_in, sem_out, *, tm):
    i = pl.program_id(0)
    f = pl.program_id(1)
    n_used = nused_ref[0]
    last_f = pl.num_programs(1) - 1
    slot = i % 2

    def gather_copy(r, src_row, s):
        return pltpu.make_async_copy(h_hbm.at[pl.ds(src_row, 1)], xbuf.at[s, pl.ds(r, 1)], sem_in.at[s])

    def scatter_copy(r, dst_row, s):
        return pltpu.make_async_copy(xbuf.at[s, pl.ds(r, 1)], y_hbm.at[pl.ds(dst_row, 1)], sem_out)

    def each_row(fn):
        def body(r, c):
            fn(r)
            return c

        lax.fori_loop(0, tm, body, 0, unroll=8)

    def gather_start(tile, s):
        each_row(lambda r: gather_copy(r, src_ref[tile * tm + r], s).start())

    def gather_wait(s):
        each_row(lambda r: gather_copy(r, 0, s).wait())

    def scatter(tile, start):
        s = tile % 2

        def one(r):
            dst = dst_ref[tile * tm + r]

            @pl.when(dst >= 0)
            def _():
                if start:
                    scatter_copy(r, dst, s).start()
                else:
                    scatter_copy(r, 0, s).wait()

        each_row(one)

    @pl.when(i < n_used)
    def _():
        @pl.when(first_ref[i] == 1)
        def _():
            wg_bf[f] = wg_ref[0].astype(BF16)
            wu_bf[f] = wu_ref[0].astype(BF16)
            wd_bf[f] = wd_ref[0].astype(BF16)

        @pl.when(f == 0)
        def _():
            @pl.when(i == 0)
            def _():
                gather_start(0, 0)

            gather_wait(slot)
            xbf[...] = xbuf[slot].astype(BF16)

        @pl.when((f == last_f) & (i + 1 < n_used))
        def _():
            gather_start(i + 1, 1 - slot)

        x = xbf[...]
        gt = _dot(x, wg_bf[f])
        up = _dot(x, wu_bf[f])
        hid = gt * _sigmoid(gt) * up
        part = _dot(hid.astype(BF16), wd_bf[f])

        @pl.when(f == 0)
        def _():
            @pl.when(i > 0)
            def _():
                scatter(i - 1, start=False)

            xbuf[slot] = part

        @pl.when(f == last_f)
        def _():
            xbuf[slot] = (xbuf[slot] + part) * roww_ref[...]
            scatter(i, start=True)

            @pl.when(i == n_used - 1)
            def _():
                scatter(i, start=False)


def _moe(h_all, route_id, route_w, w_gate, w_up, w_down):
    n, d = h_all.shape
    tm = 256
    n_ff = 2
    ff = EXPERT_FF // n_ff
    a = 2 * n
    t_max = (a + N_EXPERTS * (tm - 1)) // tm
    e_flat = route_id.reshape(-1)
    onehot = (e_flat[:, None] == jnp.arange(N_EXPERTS, dtype=I32)[None, :]).astype(I32)
    running = jnp.cumsum(onehot, axis=0)
    counts = running[-1]
    tiles_per = (counts + tm - 1) // tm
    tiles_end = jnp.cumsum(tiles_per)
    n_used = tiles_end[-1]
    seg_start = (tiles_end - tiles_per) * tm
    dest = jnp.sum(onehot * (running - 1 + seg_start[None, :]), axis=1)
    owner = jnp.zeros((t_max * tm,), I32).at[dest].set(jnp.arange(1, a + 1, dtype=I32)) - 1
    valid = owner >= 0
    src_row = jnp.where(valid, owner // 2, 0)
    dst_row = jnp.where(valid, (owner % 2) * n + owner // 2, -1)
    row_w = jnp.where(valid, route_w.reshape(-1)[jnp.maximum(owner, 0)], 0.0).reshape(-1, 1)
    tile = jnp.minimum(jnp.arange(t_max, dtype=I32), n_used - 1)
    tile_e = jnp.minimum(jnp.sum((tiles_end[None, :] <= tile[:, None]).astype(I32), axis=1), N_EXPERTS - 1)

    first = ((tiles_end - tiles_per)[tile_e] == jnp.arange(t_max, dtype=I32)).astype(I32)

    def half(i, f, fi):
        return jnp.where(fi[i] == 1, f, n_ff - 1)

    assert n_ff == 2
    grid_spec = pltpu.PrefetchScalarGridSpec(
        num_scalar_prefetch=5,
        grid=(t_max, n_ff),
        in_specs=[pl.BlockSpec(memory_space=pl.ANY),
                  pl.BlockSpec((tm, 1), lambda i, f, te, nu, fi, sr, ds: (i, 0)),
                  pl.BlockSpec((1, d, ff), lambda i, f, te, nu, fi, sr, ds: (te[i], 0, half(i, f, fi))),
                  pl.BlockSpec((1, d, ff), lambda i, f, te, nu, fi, sr, ds: (te[i], 0, half(i, f, fi))),
                  pl.BlockSpec((1, ff, d), lambda i, f, te, nu, fi, sr, ds: (te[i], half(i, f, fi), 0))],
        out_specs=pl.BlockSpec(memory_space=pl.ANY),
        scratch_shapes=[pltpu.VMEM((2, tm, d), F32), pltpu.VMEM((tm, d), BF16),
                        pltpu.VMEM((n_ff, d, ff), BF16), pltpu.VMEM((n_ff, d, ff), BF16),
                        pltpu.VMEM((n_ff, ff, d), BF16),
                        pltpu.SemaphoreType.DMA((2,)), pltpu.SemaphoreType.DMA(())],
    )
    return pl.pallas_call(
        functools.partial(_moe_kernel, tm=tm),
        out_shape=jax.ShapeDtypeStruct((a, d), F32),
        grid_spec=grid_spec,
        compiler_params=pltpu.CompilerParams(dimension_semantics=("arbitrary", "arbitrary"),
                                             vmem_limit_bytes=MOE_VMEM_LIMIT, has_side_effects=True),
        name="moe",
    )(tile_e, n_used.reshape(1).astype(I32), first, src_row, dst_row, h_all, row_w, w_gate, w_up, w_down)


def _final_kernel(x_ref, gate_ref, ya_ref, yb_ref, o_ref):
    o_ref[...] = x_ref[...] + _as2d(gate_ref[...]) * (ya_ref[...] + yb_ref[...])


def _final(x1, mod, per_token, nb, y, row_off, n_all, tm):
    m, d = x1.shape
    nt = m // nb // tm
    off_a = row_off // tm
    off_b = (n_all + row_off) // tm
    row = pl.BlockSpec((tm, d), lambda b, i: (b * nt + i, 0))
    return pl.pallas_call(
        _final_kernel,
        out_shape=jax.ShapeDtypeStruct((m, d), F32),
        grid=(nb, nt),
        in_specs=[row, _mod_spec(per_token, tm, d, 5, nt),
                  pl.BlockSpec((tm, d), lambda b, i: (off_a + b * nt + i, 0)),
                  pl.BlockSpec((tm, d), lambda b, i: (off_b + b * nt + i, 0))],
        out_specs=row,
        compiler_params=_cparams("arbitrary", "arbitrary"),
        name="final_residual",
    )(x1, mod, y, y)


def _pick_tile(m, pref):
    t = pref
    while m % t:
        t //= 2
    return t


def _project(h, w_in, gains, d):
    m = h.shape[0]
    tm = _pick_tile(m, 1024)
    tn = 512
    sbw = SB_HEADS * HEAD_DIM
    nw = NSA_HEADS * HEAD_DIM
    kvw = NSA_G * HEAD_DIM
    o_q, o_kv, o_qn, o_nkv, o_win, o_gn = 0, sbw, 3 * sbw, 3 * sbw + nw, 3 * sbw + nw + 4 * kvw, 3 * sbw + nw + 6 * kvw
    mm = functools.partial(_matmul, [h], [w_in], tm=tm, tn=tn)
    q_sb = mm(o_q, sbw, [], _ep_scaled, BF16, name="proj_q_sb")
    kv_sb = mm(o_kv, 2 * sbw, [], _ep_plain, F32, name="proj_kv_sb")

    def normed(off, width, gain_cols, flag_cols, post, dtype, name):
        return mm(off, width, [("col", gain_cols.reshape(1, width)), ("col", flag_cols.reshape(1, width))],
                  functools.partial(_ep_groupnorm, post_scale=post), dtype, name=name)

    ones = lambda k: jnp.ones((k * HEAD_DIM,), F32)
    zeros = lambda k: jnp.zeros((k * HEAD_DIM,), F32)
    q_n = normed(o_qn, nw, jnp.tile(gains["q"], NSA_HEADS), ones(NSA_HEADS), SCALE, BF16, "proj_q_nsa")
    nsa_kv = normed(o_nkv, 4 * kvw,
                    jnp.concatenate([ones(2 * NSA_G), jnp.tile(gains["ks"], NSA_G), ones(NSA_G)]),
                    jnp.concatenate([zeros(2 * NSA_G), ones(NSA_G), zeros(NSA_G)]), 1.0, F32, "proj_nsa_kv")
    win_kv = normed(o_win, 2 * kvw, jnp.concatenate([jnp.tile(gains["kw"], NSA_G), ones(NSA_G)]),
                    jnp.concatenate([ones(NSA_G), zeros(NSA_G)]), 1.0, F32, "proj_win_kv")
    g_n = mm(o_gn, tn, [], _ep_sigmoid, F32, name="proj_g_nsa")[:, :3 * NSA_HEADS]
    w_gates = w_in[:, o_gn + 3 * NSA_HEADS:]
    gate_sb = _matmul([h], [w_gates], 0, d, [], _ep_sigmoid, BF16, tm, tn, "proj_gate_sb")
    gate_n = _matmul([h], [w_gates], d, d, [], _ep_sigmoid, BF16, tm, tn, "proj_gate_nsa")
    gates = jnp.pad(g_n.reshape(m, NSA_G, 3 * NSA_REP).transpose(1, 0, 2),
                    ((0, 0), (0, 0), (0, LANES - 3 * NSA_REP)))
    return q_sb, kv_sb, q_n, nsa_kv, win_kv, gates, gate_sb, gate_n


def _compress_weights(w1):
    half = CMP_STRIDE * HEAD_DIM
    a = w1[:half].reshape(CMP_STRIDE, HEAD_DIM, CMP_HIDDEN)
    b = w1[half:].reshape(CMP_STRIDE, HEAD_DIM, CMP_HIDDEN)
    return jnp.concatenate([a, b], axis=-1).astype(BF16)


def _merge_out(o_sb, o_n, gate_sb, gate_n, w):
    m = o_sb.shape[0]
    d = w["w_out"].shape[1]
    tm = _pick_tile(m, 1024)
    mixed = _matmul([o_sb, o_n], [w["w_br_sb"], w["w_br_nsa"]], 0, d, [("tile", gate_sb), ("tile", gate_n)],
                    _ep_gated_pair, BF16, tm, 512, "merge")
    return _matmul([mixed], [w["w_out"]], 0, d, [], _ep_plain, F32, tm, 512, "out_proj")


def _alibi_slopes():
    h = jnp.arange(1, NSA_HEADS + 1, dtype=F32)
    return jnp.exp2(-8.0 * h / NSA_HEADS).reshape(NSA_G, NSA_REP)


def _layer(xp, xs, c_all, sb_cache, nsa_cache, win_cache, page_table, w, nb, t, nbs, dec):
    d = xp.shape[1]
    n_p, n_s = xp.shape[0], xs.shape[0]
    slopes = _alibi_slopes()
    mod = _adaln(c_all, w["w_ada"], w["b_ada"])
    mod_p = mod[:nb].reshape(nb, 1, 6 * d)
    mod_s = jnp.repeat(mod[nb:nb + nbs], dec, axis=0)
    gains = {"q": w["nsa_q_g"], "ks": w["nsa_ks_g"], "kw": w["nsa_kw_g"]}
    w1k, w1v = _compress_weights(w["cmp_w1_k"]), _compress_weights(w["cmp_w1_v"])
    cpp = PAGE // CMP_STRIDE

    tm_p = _pick_tile(t, 256)
    h_p = _modulate(xp, w["norm1_g"], mod_p, False, nb, 1, 0, tm_p)
    q_sb, kv_sb_p, q_n, nsa_kv_p, win_kv_p, gates, gate_sb, gate_n = _project(h_p, w["w_in"], gains, d)
    o_sb = _sb_prompt(q_sb, kv_sb_p, nb, t)
    ident = jnp.arange(nb * (t // PAGE), dtype=I32).reshape(nb, t // PAGE)
    cmp_w = (*_pos_bias(w["cmp_pos_k"], w["cmp_pos_v"], w["cmp_w1_k"], w["cmp_w1_v"]), w["cmp_w2_k"], w["cmp_w2_v"],
             w["nsa_kc_g"])
    nsa_rows_p = nsa_kv_p.reshape(nb, t, 4, NSA_G, HEAD_DIM)
    hk, hv = _compress(nsa_rows_p.reshape(nb * t // PAGE, PAGE * 4 * NSA_G, HEAD_DIM), ident, w1k, w1v,
                       _pick_tile(t // PAGE, 8))
    nc_p = t // CMP_STRIDE
    kc, vc = _cmp_finish(hk, hv, (), 0, -(-nc_p // LANES) * LANES, *cmp_w)
    o_c, sel = _cmp_select(q_n, kc, vc, slopes, nb, t, 128, t // SEL_BLOCK, 0)
    o_n = _nsa_prompt(q_n, nsa_kv_p, win_kv_p, sel, o_c, gates, slopes, nb, t)
    mix_p = _merge_out(o_sb, o_n, gate_sb, gate_n, w)

    n_pages = page_table.shape[1]
    past = n_pages * PAGE
    h_s = _modulate(xs, w["norm1_g"], mod_s, True, 1, 1, 0, n_s)
    q_sb, kv_sb_s, q_n, nsa_kv_s, win_kv_s, gates, gate_sb, gate_n = _project(h_s, w["w_in"], gains, d)
    eye = jnp.eye(SB_HEADS, dtype=BF16)
    q_rhs = (q_sb.reshape(nbs, dec, SB_HEADS, HEAD_DIM).transpose(0, 2, 3, 1)[:, :, :, None, :]
             * eye[None, :, None, :, None]).reshape(nbs, SB_HEADS // 2, 2 * HEAD_DIM, SB_HEADS * dec)
    o_sb = _sb_sample(q_rhs, kv_sb_s, sb_cache.reshape(sb_cache.shape[0], -1, HEAD_DIM), page_table, dec)
    nsa_pages = nsa_cache.reshape(nsa_cache.shape[0], -1, HEAD_DIM)
    hk, hv = _compress(nsa_pages, page_table, w1k, w1v, _pick_tile(n_pages, 8))
    nsa_rows_s = nsa_kv_s.reshape(nbs, dec, 4, NSA_G, HEAD_DIM)
    tail = jnp.pad(nsa_rows_s.reshape(nbs, dec * 4 * NSA_G, HEAD_DIM), ((0, 0), (0, (PAGE - dec) * 4 * NSA_G), (0, 0)))
    n_tot = past + dec + (-(past + dec)) % SEL_BLOCK
    nc = n_tot // CMP_STRIDE
    nc_pad = -(-nc // LANES) * LANES
    n_tail = nc - n_pages * cpp
    assert 0 <= n_tail <= cpp
    tails = _compress(tail, jnp.arange(nbs, dtype=I32).reshape(nbs, 1), w1k, w1v, 1) if n_tail else ()
    kc, vc = _cmp_finish(hk, hv, tuple(tails), n_tail, nc_pad, *cmp_w)
    o_c, sel = _cmp_select(q_n, kc, vc, slopes, nbs, dec, dec, n_tot // SEL_BLOCK, past)
    o_n = _nsa_sample(q_n, sel, o_c, gates, nsa_kv_s, win_kv_s, win_cache.reshape(nbs, -1, HEAD_DIM),
                      nsa_pages, page_table, slopes, dec)
    mix_s = _merge_out(o_sb, o_n, gate_sb, gate_n, w)

    wr = jnp.pad(jnp.concatenate([w["w_route_group"], w["w_route_expert"]], axis=1),
                 ((0, 0), (0, LANES - MOE_GROUPS - N_EXPERTS)))
    br = jnp.pad(jnp.concatenate([w["b_route_group"], w["b_route_expert"]]),
                 (0, LANES - MOE_GROUPS - N_EXPERTS)).reshape(1, LANES)
    wr_hi, wr_lo = _split(wr)
    tm_r = _pick_tile(t, 128)
    x1_p, h2_p, rid_p, rw_p = _resid_route(xp, mix_p, w["norm2_g"], mod_p, False, nb, wr_hi, wr_lo, br, tm_r)
    x1_s, h2_s, rid_s, rw_s = _resid_route(xs, mix_s, w["norm2_g"], mod_s, True, 1, wr_hi, wr_lo, br,
                                           _pick_tile(n_s, 128))
    h2 = jnp.concatenate([h2_p, h2_s], axis=0)
    rid = jnp.concatenate([rid_p[:, :2], rid_s[:, :2]], axis=0)
    rw = jnp.concatenate([rw_p[:, :2], rw_s[:, :2]], axis=0)
    y = _moe(h2, rid, rw, w["w_exp_gate"], w["w_exp_up"], w["w_exp_down"])
    n_all = n_p + n_s
    tm_f = _pick_tile(n_s, _pick_tile(t, 256))
    y_p = _final(x1_p, mod_p, False, nb, y, 0, n_all, tm_f)
    y_s = _final(x1_s, mod_s, True, 1, y, n_p, n_all, tm_f)

    win_state_p = win_kv_p.reshape(nb, t, -1)[:, t - min(WINDOW, t):]
    win_state_s = jnp.concatenate([win_cache[:, dec:], win_kv_s.reshape((nbs, dec) + win_cache.shape[2:])], axis=1)
    state = (kv_sb_p, nsa_rows_p, win_state_p, kv_sb_s, nsa_rows_s, win_state_s)
    return y_p, y_s, state


def kernel(x_prompt, x_sample, c_prompt, c_sample, cache_sb_kv, cache_nsa_kv, cache_win_kv, page_table, norm1_g, norm2_g, w_ada, b_ada, w_in, nsa_q_g, nsa_kc_g, nsa_ks_g, nsa_kw_g, cmp_pos_k, cmp_pos_v, cmp_w1_k, cmp_w2_k, cmp_w1_v, cmp_w2_v, w_br_sb, w_br_nsa, w_out, w_route_group, b_route_group, w_route_expert, b_route_expert, w_exp_gate, w_exp_up, w_exp_down):
    nb, t, d = x_prompt.shape
    nbs, dec, _ = x_sample.shape
    depth = w_in.shape[0]
    weights = dict(norm1_g=norm1_g, norm2_g=norm2_g, w_ada=w_ada, b_ada=b_ada, w_in=w_in, nsa_q_g=nsa_q_g,
                   nsa_kc_g=nsa_kc_g, nsa_ks_g=nsa_ks_g, nsa_kw_g=nsa_kw_g, cmp_pos_k=cmp_pos_k,
                   cmp_pos_v=cmp_pos_v, cmp_w1_k=cmp_w1_k, cmp_w2_k=cmp_w2_k, cmp_w1_v=cmp_w1_v,
                   cmp_w2_v=cmp_w2_v, w_br_sb=w_br_sb, w_br_nsa=w_br_nsa, w_out=w_out,
                   w_route_group=w_route_group, b_route_group=b_route_group, w_route_expert=w_route_expert,
                   b_route_expert=b_route_expert, w_exp_gate=w_exp_gate, w_exp_up=w_exp_up, w_exp_down=w_exp_down)
    n_c = nb + nbs
    c_all = jnp.pad(jnp.concatenate([c_prompt, c_sample], axis=0), ((0, (-n_c) % 8), (0, 0)))
    xp = x_prompt.reshape(nb * t, d)
    xs = x_sample.reshape(nbs * dec, d)
    states = []
    for layer in range(depth):
        w = {k: v[layer] for k, v in weights.items()}
        xp, xs, st = _layer(xp, xs, c_all, cache_sb_kv[layer], cache_nsa_kv[layer], cache_win_kv[layer],
                            page_table, w, nb, t, nbs, dec)
        states.append(st)
    g = NSA_G
    stack = lambda k, shape: jnp.stack([s[k].reshape(shape) for s in states])
    return (xp.reshape(nb, t, d), xs.reshape(nbs, dec, d),
            stack(0, (nb, t, 2, SB_HEADS, HEAD_DIM)),
            stack(1, (nb, t, 4, g, HEAD_DIM)),
            stack(2, (nb, min(WINDOW, t), 2, g, HEAD_DIM)),
            stack(3, (nbs, dec, 2, SB_HEADS, HEAD_DIM)),
            stack(4, (nbs, dec, 4, g, HEAD_DIM)),
            stack(5, (nbs, cache_win_kv.shape[2], 2, g, HEAD_DIM)))
```
